```python
import jax, jax.numpy as jnp
from jax import lax
import numpy as np

D_MODEL = 2048
BATCH = 16
SEQ = 256
DEPTH = 1
DEC_BATCH = 8
DEC_SEQ = 2048
PAST_LEN = 256

GRID_W = 64
N_DIR = 2
M_WIDTH = D_MODEL // 2
M_HEADS = 4
M_DK = M_WIDTH // M_HEADS
R_WIDTH = D_MODEL - M_WIDTH
R_N = 64
R_HEADS = R_WIDTH // R_N
LORA = 64
CONV_K = 3
CHUNK = 64
EPS = 1e-6
LNX_EPS = 64e-5
M_GATE_COLS = N_DIR * 2 * M_HEADS
SHIFT_COLS = 3 * R_WIDTH + 2 * N_DIR * LORA
IN_COLS = 5 * M_WIDTH + M_GATE_COLS + R_WIDTH + SHIFT_COLS

kernel_name = "hybrid_mlstm_rwkv7_diffusion_step"


def _split(a, sizes):
    return jnp.split(a, np.cumsum(sizes)[:-1].tolist(), axis=-1)


def rmsnorm(x, g):
    xf = x.astype(jnp.float32)
    y = xf * lax.rsqrt(jnp.mean(xf * xf, axis=-1, keepdims=True) + EPS)
    return (y * g.astype(jnp.float32)).astype(x.dtype)


def centred_conv(p, w, b):
    pad = CONV_K // 2
    T = p.shape[1]
    pp = jnp.pad(p, ((0, 0), (pad, pad), (0, 0)))
    return sum(pp[:, j:j + T] * w[j] for j in range(CONV_K)) + b


def shift_seq(p):
    B, T, C = p.shape
    p4 = p.reshape(B, T, C // 4, 4)
    prev = jnp.pad(p4, ((0, 0), (1, 0), (0, 0), (0, 0)))[:, :T]
    nxt = jnp.pad(p4, ((0, 0), (0, 1), (0, 0), (0, 0)))[:, 1:]
    sel = (jnp.arange(4) % 2) == 0
    return jnp.where(sel, prev, nxt).reshape(B, T, C)


def shift_grid(p):
    B, T, C = p.shape
    rows = T // GRID_W
    g = p.reshape(B, rows, GRID_W, C // 4, 4)
    z2 = ((0, 0), (0, 0), (1, 0), (0, 0))
    left = jnp.pad(g[..., 0], z2)[:, :, :GRID_W]
    right = jnp.pad(g[..., 1], ((0, 0), (0, 0), (0, 1), (0, 0)))[:, :, 1:]
    up = jnp.pad(g[..., 2], ((0, 0), (1, 0), (0, 0), (0, 0)))[:, :rows]
    down = jnp.pad(g[..., 3], ((0, 0), (0, 1), (0, 0), (0, 0)))[:, 1:]
    return jnp.stack([left, right, up, down], axis=-1).reshape(B, T, C)


def mlstm_scan(q, k, v, log_i, log_f, C0, n0, m0):
    B, H, T, D = q.shape
    nc = T // CHUNK

    def to_chunks(a):
        return jnp.moveaxis(a.reshape(B, H, nc, CHUNK, *a.shape[3:]), 2, 0)

    tril = jnp.tril(jnp.ones((CHUNK, CHUNK), bool))

    def step(carry, xs):
        C, n, m = carry
        qc, kc, vc, ic, fc = xs
        b = jnp.cumsum(fc, axis=-1)
        dmat = jnp.where(tril, b[..., :, None] - b[..., None, :] + ic[..., None, :], -jnp.inf)
        inter = b + m[..., None]
        mt = jnp.maximum(inter, dmat.max(-1))
        A = jnp.exp(dmat - mt[..., None]) * jnp.einsum('bhtd,bhsd->bhts', qc, kc)
        s_in = jnp.exp(inter - mt)
        num = s_in[..., None] * jnp.einsum('bhtd,bhde->bhte', qc, C) + jnp.einsum('bhts,bhse->bhte', A, vc)
        den = s_in * jnp.einsum('bhtd,bhd->bht', qc, n) + A.sum(-1)
        h = num / jnp.maximum(jnp.abs(den), jnp.exp(-mt))[..., None]
        bL = b[..., -1]
        g = bL[..., None] - b + ic
        m_new = jnp.maximum(bL + m, g.max(-1))
        decay = jnp.exp(bL + m - m_new)
        wk = jnp.exp(g - m_new[..., None])[..., None] * kc
        C_new = decay[..., None, None] * C + jnp.einsum('bhsd,bhse->bhde', wk, vc)
        n_new = decay[..., None] * n + wk.sum(-2)
        return (C_new, n_new, m_new), h

    xs = tuple(to_chunks(a) for a in (q, k, v, log_i, log_f))
    (C, n, m), hs = lax.scan(step, (C0, n0, m0), xs)
    h = jnp.moveaxis(hs, 0, 2).reshape(B, H, T, D)
    return h, C, n, m


def rwkv_scan(r, d, k, v, kk, a, S0):
    def step(S, xs):
        rt, dt, kt, vt, kkt, at = xs
        sa = jnp.einsum('bhij,bhj->bhi', S, -kkt)
        S = S * dt[:, :, None, :] + sa[..., :, None] * (kkt * at)[..., None, :] + vt[..., :, None] * kt[..., None, :]
        return S, jnp.einsum('bhij,bhj->bhi', S, rt)

    xs = tuple(jnp.moveaxis(t, 1, 0) for t in (r, d, k, v, kk, a))
    S, ys = lax.scan(step, S0, xs)
    return jnp.moveaxis(ys, 0, 1), S


def mixer(h, st, lw, grid):
    (w_in, m_conv_w, m_conv_b, m_gate_b, m_ln_g, r_mu, r_w0, r_w2, r_a0, r_a2,
     r_k_k, r_k_a, r_r_k, r_ln_g, r_ln_b, w_out) = lw
    C0, n0, m0, S0 = st
    f32 = jnp.float32
    B, T, _ = h.shape
    p = jnp.einsum('btd,dc->btc', h, w_in)
    mq, mk, mv, mo, mz, mg, rz, rs = _split(p, [M_WIDTH] * 5 + [M_GATE_COLS, R_WIDTH, SHIFT_COLS])

    qk = jax.nn.silu(centred_conv(jnp.concatenate([mq, mk], -1), m_conv_w, m_conv_b))
    mq, mk = _split(qk, [M_WIDTH, M_WIDTH])
    heads = lambda a: a.reshape(B, T, M_HEADS, M_DK).transpose(0, 2, 1, 3).astype(f32)
    q = heads(mq)
    k = heads(mk) * (M_DK ** -0.5)
    v = heads(mv)
    gates = (mg.reshape(B, T, N_DIR, 2, M_HEADS).astype(f32) + m_gate_b).transpose(2, 3, 0, 4, 1)
    log_i = gates[:, 0]
    log_f = jax.nn.log_sigmoid(gates[:, 1])
    fl = lambda a: jnp.flip(a, axis=2)
    c32 = lambda a: a.astype(f32)
    h_f, Cf, nf, mf = mlstm_scan(q, k, v, log_i[0], log_f[0], c32(C0[:, 0]), c32(n0[:, 0]), c32(m0[:, 0]))
    h_b, Cb, nb, mb = mlstm_scan(fl(q), fl(k), fl(v), fl(log_i[1]), fl(log_f[1]),
                                 c32(C0[:, 1]), c32(n0[:, 1]), c32(m0[:, 1]))
    hm = jax.nn.sigmoid(heads(mo)) * (h_f + fl(h_b))
    mu = hm.mean(-1, keepdims=True)
    hm = (hm - mu) * lax.rsqrt(jnp.mean((hm - mu) ** 2, axis=-1, keepdims=True) + EPS)
    hm = hm.transpose(0, 2, 1, 3).reshape(B, T, M_WIDTH) * m_ln_g * jax.nn.silu(mz.astype(f32))

    shift = shift_grid if grid else shift_seq
    rs = rs + r_mu * (shift(rs) - rs)
    rr, rk, rv, rwd, rad = _split(rs, [R_WIDTH] * 3 + [N_DIR * LORA, N_DIR * LORA])
    rh = lambda a: a.reshape(B, T, R_HEADS, R_N).astype(f32)
    r = rh(rr)
    vv = rh(rv)
    kraw = rk.astype(f32)
    rwd = rwd.reshape(B, T, N_DIR, LORA).astype(f32)
    rad = rad.reshape(B, T, N_DIR, LORA).astype(f32)
    wlog = -jax.nn.softplus(-(r_w0 + jnp.einsum('btzl,zlc->btzc', jnp.tanh(rwd), r_w2))) - 0.5
    decay = jnp.exp(-jnp.exp(wlog))
    a = jax.nn.sigmoid(r_a0 + jnp.einsum('btzl,zlc->btzc', rad, r_a2))
    kk = rh(kraw * r_k_k)
    kk = kk / jnp.maximum(jnp.sqrt(jnp.sum(kk * kk, axis=-1, keepdims=True)), 1e-12)
    kz = kraw[:, :, None, :] * (1.0 + (a - 1.0) * r_k_a)
    ft = lambda t: jnp.flip(t, axis=1)
    y_f, Sf = rwkv_scan(r, rh(decay[:, :, 0]), rh(kz[:, :, 0]), vv, kk, rh(a[:, :, 0]), c32(S0[:, 0]))
    y_b, Sb = rwkv_scan(ft(r), ft(rh(decay[:, :, 1])), ft(rh(kz[:, :, 1])), ft(vv), ft(kk),
                        ft(rh(a[:, :, 1])), c32(S0[:, 1]))
    y = y_f + ft(y_b)
    ym = y.mean(-1, keepdims=True)
    y = (y - ym) * lax.rsqrt(jnp.mean((y - ym) ** 2, axis=-1, keepdims=True) + LNX_EPS)
    y = y.reshape(B, T, R_WIDTH) * r_ln_g + r_ln_b
    bonus = jnp.einsum('bthn,btzhn,hn->bth', r, kz.reshape(B, T, N_DIR, R_HEADS, R_N), r_r_k)
    y = y + (bonus[..., None] * vv).reshape(B, T, R_WIDTH)
    y = y * jax.nn.silu(rz.astype(f32))

    cat = jnp.concatenate([hm, y], axis=-1).astype(h.dtype)
    out = jnp.einsum('btc,cd->btd', cat, w_out)
    new_st = (jnp.stack([Cf, Cb], 1), jnp.stack([nf, nb], 1), jnp.stack([mf, mb], 1), jnp.stack([Sf, Sb], 1))
    return out, new_st


def block(x, mod, st, lw, norm_g, grid):
    shift, scale, gate = jnp.split(mod, 3, axis=-1)
    h = rmsnorm(x, norm_g) * (1.0 + scale) + shift
    out, new_st = mixer(h, st, lw, grid)
    return x + gate * out, new_st


def setup_inputs(seed: int = 0) -> dict:
    key = jax.random.key(seed)
    ks = jax.random.split(key, 40)
    f32 = jnp.float32
    nrm = lambda k, shape, s: s * jax.random.normal(k, shape, f32)
    L = DEPTH
    return {
        "x_prompt": nrm(ks[0], (BATCH, SEQ, D_MODEL), 1.0),
        "x_sample": nrm(ks[1], (DEC_BATCH, DEC_SEQ, D_MODEL), 1.0),
        "state_mlstm_C": nrm(ks[2], (DEC_BATCH, L, N_DIR, M_HEADS, M_DK, M_DK), 0.05),
        "state_mlstm_n": nrm(ks[3], (DEC_BATCH, L, N_DIR, M_HEADS, M_DK), 0.5),
        "state_mlstm_m": nrm(ks[4], (DEC_BATCH, L, N_DIR, M_HEADS), 1.0),
        "state_rwkv_S": nrm(ks[5], (DEC_BATCH, L, N_DIR, R_HEADS, R_N, R_N), 0.1),
        "c": nrm(ks[6], (DEC_BATCH, D_MODEL), 1.0),
        "c_ctx": nrm(ks[7], (D_MODEL,), 1.0),
        "norm_g": 1.0 + nrm(ks[8], (L, D_MODEL), 0.02),
        "w_ada": nrm(ks[9], (L, D_MODEL, 3 * D_MODEL), 0.5 * D_MODEL ** -0.5),
        "b_ada": nrm(ks[10], (L, 3 * D_MODEL), 0.02),
        "w_in": nrm(ks[11], (L, D_MODEL, IN_COLS), D_MODEL ** -0.5),
        "m_conv_w": nrm(ks[12], (L, CONV_K, 2 * M_WIDTH), CONV_K ** -0.5),
        "m_conv_b": nrm(ks[13], (L, 2 * M_WIDTH), 0.02),
        "m_gate_b": jnp.stack([nrm(ks[14], (L, N_DIR, M_HEADS), 0.1),
                               jnp.linspace(3.0, 6.0, M_HEADS, dtype=f32) + nrm(ks[15], (L, N_DIR, M_HEADS), 0.1)],
                              axis=2),
        "m_ln_g": 1.0 + nrm(ks[16], (L, M_WIDTH), 0.02),
        "r_mu": jax.random.uniform(ks[17], (L, SHIFT_COLS), f32),
        "r_w0": jax.random.uniform(ks[18], (L, N_DIR, R_WIDTH), f32, -6.0, 0.0),
        "r_w2": nrm(ks[19], (L, N_DIR, LORA, R_WIDTH), 0.1 * LORA ** -0.5),
        "r_a0": nrm(ks[20], (L, N_DIR, R_WIDTH), 0.1),
        "r_a2": nrm(ks[21], (L, N_DIR, LORA, R_WIDTH), 0.1 * LORA ** -0.5),
        "r_k_k": 0.85 + nrm(ks[22], (L, R_WIDTH), 0.02),
        "r_k_a": 1.0 + nrm(ks[23], (L, R_WIDTH), 0.02),
        "r_r_k": nrm(ks[24], (L, R_HEADS, R_N), 0.1),
        "r_ln_g": 1.0 + nrm(ks[25], (L, R_WIDTH), 0.02),
        "r_ln_b": nrm(ks[26], (L, R_WIDTH), 0.02),
        "w_out": nrm(ks[27], (L, D_MODEL, D_MODEL), D_MODEL ** -0.5),
        "final_g": 1.0 + nrm(ks[28], (D_MODEL,), 0.02),
    }


def reference(x_prompt, x_sample, state_mlstm_C, state_mlstm_n, state_mlstm_m, state_rwkv_S, c, c_ctx,
              norm_g, w_ada, b_ada, w_in, m_conv_w, m_conv_b, m_gate_b, m_ln_g, r_mu, r_w0, r_w2, r_a0,
              r_a2, r_k_k, r_k_a, r_r_k, r_ln_g, r_ln_b, w_out, final_g):
    f32 = jnp.float32
    bp = x_prompt.shape[0]
    ctx_state0 = (jnp.zeros((bp, N_DIR, M_HEADS, M_DK, M_DK), f32),
                  jnp.zeros((bp, N_DIR, M_HEADS, M_DK), f32),
                  jnp.full((bp, N_DIR, M_HEADS), -jnp.inf, f32),
                  jnp.zeros((bp, N_DIR, R_HEADS, R_N, R_N), f32))
    xp, xs = x_prompt, x_sample
    new_C, new_n, new_m, new_S = [], [], [], []
    for l in range(DEPTH):
        lw = (w_in[l], m_conv_w[l], m_conv_b[l], m_gate_b[l], m_ln_g[l], r_mu[l], r_w0[l], r_w2[l],
              r_a0[l], r_a2[l], r_k_k[l], r_k_a[l], r_r_k[l], r_ln_g[l], r_ln_b[l], w_out[l])
        mod_p = (jax.nn.silu(c_ctx) @ w_ada[l] + b_ada[l])[None, None, :]
        mod_s = (jax.nn.silu(c) @ w_ada[l] + b_ada[l])[:, None, :]
        xp, (Cp, np_, mp, Sp) = block(xp, mod_p, ctx_state0, lw, norm_g[l], False)
        st_s = (state_mlstm_C[:, l], state_mlstm_n[:, l], state_mlstm_m[:, l], state_rwkv_S[:, l])
        xs, _ = block(xs, mod_s, st_s, lw, norm_g[l], True)
        new_C.append(Cp)
        new_n.append(np_)
        new_m.append(mp)
        new_S.append(Sp)
    y_prompt = rmsnorm(xp, final_g)
    y_sample = rmsnorm(xs, final_g)
    return (y_prompt, y_sample, jnp.stack(new_C, 1), jnp.stack(new_n, 1), jnp.stack(new_m, 1), jnp.stack(new_S, 1))
```

```python
import functools

import jax
import jax.numpy as jnp
import numpy as np
from jax import lax
from jax.experimental import pallas as pl
from jax.experimental.pallas import tpu as pltpu

D_MODEL = 2048
GRID_W = 64
N_DIR = 2
M_WIDTH = D_MODEL // 2
M_HEADS = 4
M_DK = M_WIDTH // M_HEADS
R_WIDTH = D_MODEL - M_WIDTH
R_N = 64
R_HEADS = R_WIDTH // R_N
LORA = 64
CONV_K = 3
CHUNK = 64
EPS = 1e-6
LNX_EPS = 64e-5
M_GATE_COLS = N_DIR * 2 * M_HEADS
SHIFT_COLS = 3 * R_WIDTH + 2 * N_DIR * LORA

LANES = 128
GATE_PAD = LANES
MAIN_COLS = 5 * M_WIDTH + R_WIDTH + SHIFT_COLS
PROJ_COLS = MAIN_COLS + GATE_PAD
VMEM_LIMIT = 48 * 1024 * 1024

F32 = jnp.float32
BF16 = jnp.bfloat16


def _params(n_axes):
    return pltpu.CompilerParams(dimension_semantics=("arbitrary",) * n_axes,
                                vmem_limit_bytes=VMEM_LIMIT)


def _mm_kernel(x_ref, w_ref, o_ref):
    o_ref[...] = jnp.dot(x_ref[...].astype(BF16), w_ref[...].astype(BF16),
                         preferred_element_type=F32)


def _mm(x, w, tm, tn):
    M, K = x.shape
    N = w.shape[1]
    assert M % tm == 0 and N % tn == 0
    return pl.pallas_call(
        _mm_kernel,
        grid=(N // tn, M // tm),
        in_specs=[pl.BlockSpec((tm, K), lambda j, i: (i, 0)),
                  pl.BlockSpec((K, tn), lambda j, i: (0, j))],
        out_specs=pl.BlockSpec((tm, tn), lambda j, i: (i, j)),
        out_shape=jax.ShapeDtypeStruct((M, N), F32),
        compiler_params=_params(2),
        name="proj_mm",
    )(x, w)


def _mlstm_kernel(q_ref, k_ref, v_ref, gi_ref, gf_ref, c0_ref, n0_ref, m0_ref,
                  h_ref, c_ref, n_ref, m_ref):
    dirn = pl.program_id(2)
    nc = gi_ref.shape[0]
    L = CHUNK
    sgn = 1 - 2 * dirn
    row = lax.broadcasted_iota(jnp.int32, (L, L), 0)
    col = lax.broadcasted_iota(jnp.int32, (L, L), 1)
    seen = (row - col) * sgn >= 0
    seen_t = (col - row) * sgn >= 0
    eye = row == col
    neg_inf = jnp.float32(-jnp.inf)

    c_ref[...] = c0_ref[...]

    def body(ci, carry):
        n, m = carry
        cc = ci + dirn * (nc - 1 - 2 * ci)
        r0 = pl.multiple_of(cc * L, L)
        q = q_ref[pl.ds(r0, L), :]
        k = k_ref[pl.ds(r0, L), :]
        v = v_ref[pl.ds(r0, L), :]
        f_row = gf_ref[pl.ds(cc, 1), :]
        i_row = gi_ref[pl.ds(cc, 1), :]
        f_col = jnp.sum(jnp.where(eye, f_row, 0.0), axis=1, keepdims=True)
        i_col = jnp.sum(jnp.where(eye, i_row, 0.0), axis=1, keepdims=True)
        b_col = jnp.sum(jnp.where(seen, f_row, 0.0), axis=1, keepdims=True)
        b_row = jnp.sum(jnp.where(seen_t, f_col, 0.0), axis=0, keepdims=True)
        dmat = jnp.where(seen, b_col - b_row + i_row, neg_inf)
        inter = b_col + m
        mt = jnp.maximum(inter, jnp.max(dmat, axis=1, keepdims=True))
        qb = q.astype(BF16)
        kb = k.astype(BF16)
        vb = v.astype(BF16)
        qk = lax.dot_general(qb, kb, (((1,), (1,)), ((), ())), preferred_element_type=F32)
        a = jnp.exp(dmat - mt) * qk
        s_in = jnp.exp(inter - mt)
        c = c_ref[...]
        qc = jnp.dot(qb, c.astype(BF16), preferred_element_type=F32)
        av = jnp.dot(a.astype(BF16), vb, preferred_element_type=F32)
        num = s_in * qc + av
        den = s_in * jnp.sum(q * n, axis=1, keepdims=True) + jnp.sum(a, axis=1, keepdims=True)
        h_ref[pl.ds(r0, L), :] = num / jnp.maximum(jnp.abs(den), jnp.exp(-mt))
        b_last = jnp.sum(f_row, axis=1, keepdims=True)
        g_row = b_last - b_row + i_row
        g_col = b_last - b_col + i_col
        m_new = jnp.maximum(b_last + m, jnp.max(g_row, axis=1, keepdims=True))
        decay = jnp.exp(b_last + m - m_new)
        wk = jnp.exp(g_col - m_new) * k
        c_ref[...] = decay * c + lax.dot_general(
            wk.astype(BF16), vb, (((0,), (0,)), ((), ())), preferred_element_type=F32)
        n_new = decay * n + jnp.sum(wk, axis=0, keepdims=True)
        return n_new, m_new

    n, m = lax.fori_loop(0, nc, body, (n0_ref[...], m0_ref[:, 0:1]))
    n_ref[...] = n
    m_ref[...] = jnp.broadcast_to(m, m_ref.shape)


def _mlstm(q, k, v, gi, gf, c0, n0, m0):
    B, T, _ = q.shape
    nc = T // CHUNK
    qkv_spec = pl.BlockSpec((None, T, M_DK), lambda b, h, d: (b, 0, h))
    gate_spec = pl.BlockSpec((None, None, None, nc, CHUNK), lambda b, h, d: (d, b, h, 0, 0))
    c_spec = pl.BlockSpec((None, None, None, M_DK, M_DK), lambda b, h, d: (b, d, h, 0, 0))
    n_spec = pl.BlockSpec((None, None, None, 1, M_DK), lambda b, h, d: (b, d, h, 0, 0))
    m_spec = pl.BlockSpec((None, None, None, 1, LANES), lambda b, h, d: (b, d, h, 0, 0))
    return pl.pallas_call(
        _mlstm_kernel,
        grid=(B, M_HEADS, N_DIR),
        in_specs=[qkv_spec, qkv_spec, qkv_spec, gate_spec, gate_spec, c_spec, n_spec, m_spec],
        out_specs=[pl.BlockSpec((None, None, T, M_DK), lambda b, h, d: (d, b, 0, h)),
                   c_spec, n_spec, m_spec],
        out_shape=[jax.ShapeDtypeStruct((N_DIR, B, T, M_WIDTH), F32),
                   jax.ShapeDtypeStruct((B, N_DIR, M_HEADS, M_DK, M_DK), F32),
                   jax.ShapeDtypeStruct((B, N_DIR, M_HEADS, 1, M_DK), F32),
                   jax.ShapeDtypeStruct((B, N_DIR, M_HEADS, 1, LANES), F32)],
        compiler_params=_params(3),
        name="mlstm_scan",
    )(q, k, v, gi, gf, c0, n0, m0)


RWKV_TB = 16


def _rwkv_kernel(r_ref, kk_ref, v_ref, d_ref, k_ref, b_ref, s0_ref, y_ref, s_ref):
    dirn = pl.program_id(0)
    tb = pl.program_id(2)
    n_tok = r_ref.shape[0]

    @pl.when(tb == 0)
    def _():
        s_ref[...] = s0_ref[...]

    def step(t, carry):
        tt = t + dirn * (n_tok - 1 - 2 * t)
        u = jnp.zeros((R_N, LANES), F32)
        for j in range(R_N):
            u = u + s_ref[j] * kk_ref[tt, j:j + 1, :]
        sa = -u
        vt = v_ref[tt]
        y = jnp.zeros((R_N, LANES), F32)
        for j in range(R_N):
            sn = (s_ref[j] * d_ref[tt, j:j + 1, :] + sa * b_ref[tt, j:j + 1, :]
                  + vt * k_ref[tt, j:j + 1, :])
            s_ref[j] = sn
            y = y + sn * r_ref[tt, j:j + 1, :]
        y_ref[tt] = y
        return carry

    lax.fori_loop(0, n_tok, step, 0)


def _rwkv(r, kk, v, d, k, b, s0):
    T, _, CH = r.shape
    nt = T // RWKV_TB
    rev = lambda dd, t: t + dd * (nt - 1 - 2 * t)
    shared = pl.BlockSpec((RWKV_TB, R_N, LANES), lambda dd, g, t: (rev(dd, t), 0, g))
    per_dir = pl.BlockSpec((None, RWKV_TB, R_N, LANES), lambda dd, g, t: (dd, rev(dd, t), 0, g))
    state = pl.BlockSpec((None, R_N, R_N, LANES), lambda dd, g, t: (dd, 0, 0, g))
    return pl.pallas_call(
        _rwkv_kernel,
        grid=(N_DIR, CH // LANES, nt),
        in_specs=[shared, shared, shared, per_dir, per_dir, per_dir, state],
        out_specs=[per_dir, state],
        out_shape=[jax.ShapeDtypeStruct((N_DIR, T, R_N, CH), F32),
                   jax.ShapeDtypeStruct((N_DIR, R_N, R_N, CH), F32)],
        compiler_params=_params(3),
        name="rwkv_scan",
    )(r, kk, v, d, k, b, s0)


OUT_TM = 512


def _out_kernel(cat_ref, w_ref, x_ref, gate_ref, g_ref, y_ref):
    out = jnp.dot(cat_ref[...], w_ref[...], preferred_element_type=F32)
    z = x_ref[...] + gate_ref[...] * out
    y_ref[...] = z * lax.rsqrt(jnp.mean(z * z, axis=-1, keepdims=True) + EPS) * g_ref[...]


def _out_proj(cat, w_out, x, gate, final_g):
    M, D = x.shape
    rows_per_gate = M // gate.shape[0]
    tiles_per_gate = rows_per_gate // OUT_TM
    return pl.pallas_call(
        _out_kernel,
        grid=(M // OUT_TM,),
        in_specs=[pl.BlockSpec((OUT_TM, D), lambda i: (i, 0)),
                  pl.BlockSpec((D, D), lambda i: (0, 0)),
                  pl.BlockSpec((OUT_TM, D), lambda i: (i, 0)),
                  pl.BlockSpec((None, 1, D), lambda i: (i // tiles_per_gate, 0, 0)),
                  pl.BlockSpec((1, D), lambda i: (0, 0))],
        out_specs=pl.BlockSpec((OUT_TM, D), lambda i: (i, 0)),
        out_shape=jax.ShapeDtypeStruct((M, D), F32),
        compiler_params=_params(1),
        name="out_proj",
    )(cat, w_out, x, gate, final_g)


def _rmsnorm(x, g):
    return x * lax.rsqrt(jnp.mean(x * x, axis=-1, keepdims=True) + EPS) * g


def _centred_conv(p, w, b):
    pad = CONV_K // 2
    T = p.shape[1]
    pp = jnp.pad(p, ((0, 0), (pad, pad), (0, 0)))
    return sum(pp[:, j:j + T] * w[j] for j in range(CONV_K)) + b


def _shift_seq(p):
    B, T, C = p.shape
    p4 = p.reshape(B, T, C // 4, 4)
    prev = jnp.pad(p4, ((0, 0), (1, 0), (0, 0), (0, 0)))[:, :T]
    nxt = jnp.pad(p4, ((0, 0), (0, 1), (0, 0), (0, 0)))[:, 1:]
    sel = (jnp.arange(4) % 2) == 0
    return jnp.where(sel, prev, nxt).reshape(B, T, C)


def _shift_grid(p):
    B, T, C = p.shape
    rows = T // GRID_W
    g = p.reshape(B, rows, GRID_W, C // 4, 4)
    left = jnp.pad(g[..., 0], ((0, 0), (0, 0), (1, 0), (0, 0)))[:, :, :GRID_W]
    right = jnp.pad(g[..., 1], ((0, 0), (0, 0), (0, 1), (0, 0)))[:, :, 1:]
    up = jnp.pad(g[..., 2], ((0, 0), (1, 0), (0, 0), (0, 0)))[:, :rows]
    down = jnp.pad(g[..., 3], ((0, 0), (0, 1), (0, 0), (0, 0)))[:, 1:]
    return jnp.stack([left, right, up, down], axis=-1).reshape(B, T, C)


def _to_chains(a):
    B, T, _ = a.shape
    return a.reshape(B, T, R_HEADS, R_N).transpose(1, 3, 0, 2).reshape(T, R_N, B * R_HEADS)


def _to_chains_dir(a):
    B, T = a.shape[:2]
    return (a.reshape(B, T, N_DIR, R_HEADS, R_N).transpose(2, 1, 4, 0, 3)
            .reshape(N_DIR, T, R_N, B * R_HEADS))


def _block(x, mod, st, lw, norm_g, final_g, grid):
    (w_proj, m_conv_w, m_conv_b, m_gate_b, m_ln_g, r_mu, r_w0, r_w2, r_a0, r_a2,
     r_k_k, r_k_a, r_r_k, r_ln_g, r_ln_b, w_out) = lw
    C0, n0, m0, S0 = st
    B, T, D = x.shape
    shift, scale, gate = jnp.split(mod, 3, axis=-1)
    h = (_rmsnorm(x, norm_g) * (1.0 + scale) + shift).astype(BF16)
    p = _mm(h.reshape(B * T, D), w_proj, 512, 1920).reshape(B, T, PROJ_COLS)
    mq, mk, mv, mo, mz = (p[..., i * M_WIDTH:(i + 1) * M_WIDTH] for i in range(5))
    rz = p[..., 5 * M_WIDTH:5 * M_WIDTH + R_WIDTH]
    rs = p[..., 5 * M_WIDTH + R_WIDTH:MAIN_COLS]
    mg = p[..., MAIN_COLS:MAIN_COLS + M_GATE_COLS]

    qk = jax.nn.silu(_centred_conv(jnp.concatenate([mq, mk], -1), m_conv_w, m_conv_b))
    q = qk[..., :M_WIDTH]
    k = qk[..., M_WIDTH:] * (M_DK ** -0.5)
    gates = (mg.reshape(B, T, N_DIR, 2, M_HEADS) + m_gate_b).transpose(2, 3, 0, 4, 1)
    nc = T // CHUNK
    log_i = gates[:, 0].reshape(N_DIR, B, M_HEADS, nc, CHUNK)
    log_f = jax.nn.log_sigmoid(gates[:, 1]).reshape(N_DIR, B, M_HEADS, nc, CHUNK)
    h_dirs, Cn, nn, mn = _mlstm(
        q, k, mv, log_i, log_f, C0, n0[..., None, :],
        jnp.broadcast_to(m0[..., None, None], m0.shape + (1, LANES)))
    hm = jax.nn.sigmoid(mo) * (h_dirs[0] + h_dirs[1])
    hm = hm.reshape(B, T, M_HEADS, M_DK)
    mu = hm.mean(-1, keepdims=True)
    hm = (hm - mu) * lax.rsqrt(jnp.mean((hm - mu) ** 2, axis=-1, keepdims=True) + EPS)
    hm = hm.reshape(B, T, M_WIDTH) * m_ln_g * jax.nn.silu(mz)

    shift_fn = _shift_grid if grid else _shift_seq
    rs = rs + r_mu * (shift_fn(rs) - rs)
    rr = rs[..., :R_WIDTH]
    kraw = rs[..., R_WIDTH:2 * R_WIDTH]
    rv = rs[..., 2 * R_WIDTH:3 * R_WIDTH]
    rwd = rs[..., 3 * R_WIDTH:3 * R_WIDTH + N_DIR * LORA].reshape(B, T, N_DIR, LORA)
    rad = rs[..., 3 * R_WIDTH + N_DIR * LORA:].reshape(B, T, N_DIR, LORA)
    lora_w = jnp.stack([_mm(jnp.tanh(rwd[:, :, z]).reshape(B * T, LORA), r_w2[z], 2048, R_WIDTH)
                        for z in range(N_DIR)], axis=1).reshape(B, T, N_DIR, R_WIDTH)
    lora_a = jnp.stack([_mm(rad[:, :, z].reshape(B * T, LORA), r_a2[z], 2048, R_WIDTH)
                        for z in range(N_DIR)], axis=1).reshape(B, T, N_DIR, R_WIDTH)
    wlog = -jax.nn.softplus(-(r_w0 + lora_w)) - 0.5
    decay = jnp.exp(-jnp.exp(wlog))
    a = jax.nn.sigmoid(r_a0 + lora_a)
    kk = (kraw * r_k_k).reshape(B, T, R_HEADS, R_N)
    kk = kk / jnp.maximum(jnp.sqrt(jnp.sum(kk * kk, axis=-1, keepdims=True)), 1e-12)
    kk = kk.reshape(B, T, R_WIDTH)
    kz = kraw[:, :, None, :] * (1.0 + (a - 1.0) * r_k_a)
    bb = kk[:, :, None, :] * a
    y_dirs, S_new = _rwkv(_to_chains(rr), _to_chains(kk), _to_chains(rv),
                          _to_chains_dir(decay), _to_chains_dir(kz), _to_chains_dir(bb),
                          S0.transpose(1, 4, 3, 0, 2).reshape(N_DIR, R_N, R_N, B * R_HEADS))
    y = (y_dirs[0] + y_dirs[1]).reshape(T, R_N, B, R_HEADS).transpose(2, 0, 3, 1)
    ym = y.mean(-1, keepdims=True)
    y = (y - ym) * lax.rsqrt(jnp.mean((y - ym) ** 2, axis=-1, keepdims=True) + LNX_EPS)
    y = y.reshape(B, T, R_WIDTH) * r_ln_g + r_ln_b
    r4 = rr.reshape(B, T, R_HEADS, R_N)
    bonus = jnp.einsum('bthn,btzhn,hn->bth', r4, kz.reshape(B, T, N_DIR, R_HEADS, R_N), r_r_k)
    y = y + (bonus[..., None] * rv.reshape(B, T, R_HEADS, R_N)).reshape(B, T, R_WIDTH)
    y = y * jax.nn.silu(rz)

    cat = jnp.concatenate([hm, y], axis=-1).astype(BF16)
    out = _out_proj(cat.reshape(B * T, D), w_out, x.reshape(B * T, D), gate,
                    final_g.reshape(1, D)).reshape(B, T, D)
    S_new = S_new.reshape(N_DIR, R_N, R_N, B, R_HEADS).transpose(3, 0, 4, 2, 1)
    return out, (Cn, nn[..., 0, :], mn[..., 0, 0], S_new)


def kernel(x_prompt, x_sample, state_mlstm_C, state_mlstm_n, state_mlstm_m, state_rwkv_S, c, c_ctx,
           norm_g, w_ada, b_ada, w_in, m_conv_w, m_conv_b, m_gate_b, m_ln_g, r_mu, r_w0, r_w2, r_a0,
           r_a2, r_k_k, r_k_a, r_r_k, r_ln_g, r_ln_b, w_out, final_g):
    depth = w_in.shape[0]
    assert depth == 1, "the final RMSNorm is fused into the single layer's output projection"
    bp = x_prompt.shape[0]
    ctx_state0 = (jnp.zeros((bp, N_DIR, M_HEADS, M_DK, M_DK), F32),
                  jnp.zeros((bp, N_DIR, M_HEADS, M_DK), F32),
                  jnp.full((bp, N_DIR, M_HEADS), -jnp.inf, F32),
                  jnp.zeros((bp, N_DIR, R_HEADS, R_N, R_N), F32))
    l = 0
    w_l = w_in[l]
    gate_lo = 5 * M_WIDTH
    gate_hi = gate_lo + M_GATE_COLS
    w_proj = jnp.concatenate(
        [w_l[:, :gate_lo], w_l[:, gate_hi:], w_l[:, gate_lo:gate_hi],
         jnp.zeros((D_MODEL, GATE_PAD - M_GATE_COLS), F32)], axis=1).astype(BF16)
    lw = (w_proj, m_conv_w[l], m_conv_b[l], m_gate_b[l], m_ln_g[l], r_mu[l], r_w0[l], r_w2[l],
          r_a0[l], r_a2[l], r_k_k[l], r_k_a[l], r_r_k[l], r_ln_g[l], r_ln_b[l], w_out[l].astype(BF16))
    n_cond = 1 + c.shape[0]
    cond = jnp.concatenate([c_ctx[None, :], c, jnp.zeros((16 - n_cond, D_MODEL), F32)], axis=0)
    mod = _mm(jax.nn.silu(cond), w_ada[l], 16, 1536)[:n_cond] + b_ada[l]
    mod_p = mod[0:1, None, :]
    mod_s = mod[1:, None, :]
    y_prompt, (Cp, np_, mp, Sp) = _block(x_prompt, mod_p, ctx_state0, lw, norm_g[l], final_g, False)
    st_s = (state_mlstm_C[:, l], state_mlstm_n[:, l], state_mlstm_m[:, l], state_rwkv_S[:, l])
    y_sample, _ = _block(x_sample, mod_s, st_s, lw, norm_g[l], final_g, True)
    return (y_prompt, y_sample, Cp[:, None], np_[:, None], mp[:, None], Sp[:, None])
```

```python
import functools

import jax
import jax.numpy as jnp
import numpy as np
from jax import lax
from jax.experimental import pallas as pl
from jax.experimental.pallas import tpu as pltpu

D_MODEL = 2048
GRID_W = 64
N_DIR = 2
M_WIDTH = D_MODEL // 2
M_HEADS = 4
M_DK = M_WIDTH // M_HEADS
R_WIDTH = D_MODEL - M_WIDTH
R_N = 64
R_HEADS = R_WIDTH // R_N
LORA = 64
CONV_K = 3
CHUNK = 64
EPS = 1e-6
LNX_EPS = 64e-5
M_GATE_COLS = N_DIR * 2 * M_HEADS
SHIFT_COLS = 3 * R_WIDTH + 2 * N_DIR * LORA

LANES = 128
SUBLANES = 8
HEAD_PAIRS = R_WIDTH // LANES
GATE_PAD = 2 * LANES
M_PROJ_COLS = 5 * M_WIDTH + GATE_PAD
R_PROJ_COLS = R_WIDTH + SHIFT_COLS
RZ_BLK0 = 0
RS_BLK0 = R_WIDTH // LANES
VMEM_LIMIT = 48 * 1024 * 1024

F32 = jnp.float32
BF16 = jnp.bfloat16


def _params(n_axes):
    return pltpu.CompilerParams(dimension_semantics=("arbitrary",) * n_axes,
                                vmem_limit_bytes=VMEM_LIMIT)


def _sigmoid(x):
    return 1.0 / (1.0 + jnp.exp(-x))


def _silu(x):
    return x * _sigmoid(x)


def _softplus(x):
    return jnp.maximum(x, 0.0) + jnp.log(1.0 + jnp.exp(-jnp.abs(x)))


def _mm_kernel(x_ref, w_ref, o_ref):
    o_ref[...] = jnp.dot(x_ref[...].astype(BF16), w_ref[...].astype(BF16),
                         preferred_element_type=F32)


def _mm(x, w, tm, tn):
    M, K = x.shape
    N = w.shape[1]
    assert M % tm == 0 and N % tn == 0
    return pl.pallas_call(
        _mm_kernel,
        grid=(N // tn, M // tm),
        in_specs=[pl.BlockSpec((tm, K), lambda j, i: (i, 0)),
                  pl.BlockSpec((K, tn), lambda j, i: (0, j))],
        out_specs=pl.BlockSpec((tm, tn), lambda j, i: (i, j)),
        out_shape=jax.ShapeDtypeStruct((M, N), F32),
        compiler_params=_params(2),
        name="proj_mm",
    )(x, w)


def _mlstm_kernel(q_ref, k_ref, v_ref, gi_ref, gf_ref, c0_ref, n0_ref, m0_ref,
                  h_ref, c_ref, n_ref, m_ref):
    dirn = pl.program_id(2)
    nc = gi_ref.shape[0]
    L = CHUNK
    sgn = 1 - 2 * dirn
    row = lax.broadcasted_iota(jnp.int32, (L, L), 0)
    col = lax.broadcasted_iota(jnp.int32, (L, L), 1)
    seen = (row - col) * sgn >= 0
    seen_t = (col - row) * sgn >= 0
    eye = row == col
    neg_inf = jnp.float32(-jnp.inf)

    c_ref[...] = c0_ref[...]

    def body(ci, carry):
        n, m = carry
        cc = ci + dirn * (nc - 1 - 2 * ci)
        r0 = pl.multiple_of(cc * L, L)
        q = q_ref[pl.ds(r0, L), :]
        k = k_ref[pl.ds(r0, L), :]
        v = v_ref[pl.ds(r0, L), :]
        f_row = gf_ref[pl.ds(cc, 1), :]
        i_row = gi_ref[pl.ds(cc, 1), :]
        f_col = jnp.sum(jnp.where(eye, f_row, 0.0), axis=1, keepdims=True)
        i_col = jnp.sum(jnp.where(eye, i_row, 0.0), axis=1, keepdims=True)
        b_col = jnp.sum(jnp.where(seen, f_row, 0.0), axis=1, keepdims=True)
        b_row = jnp.sum(jnp.where(seen_t, f_col, 0.0), axis=0, keepdims=True)
        dmat = jnp.where(seen, b_col - b_row + i_row, neg_inf)
        inter = b_col + m
        mt = jnp.maximum(inter, jnp.max(dmat, axis=1, keepdims=True))
        qb = q.astype(BF16)
        kb = k.astype(BF16)
        vb = v.astype(BF16)
        qk = lax.dot_general(qb, kb, (((1,), (1,)), ((), ())), preferred_element_type=F32)
        a = jnp.exp(dmat - mt) * qk
        s_in = jnp.exp(inter - mt)
        c = c_ref[...]
        qc = jnp.dot(qb, c.astype(BF16), preferred_element_type=F32)
        av = jnp.dot(a.astype(BF16), vb, preferred_element_type=F32)
        num = s_in * qc + av
        den = s_in * jnp.sum(q * n, axis=1, keepdims=True) + jnp.sum(a, axis=1, keepdims=True)
        h_ref[pl.ds(r0, L), :] = num / jnp.maximum(jnp.abs(den), jnp.exp(-mt))
        b_last = jnp.sum(f_row, axis=1, keepdims=True)
        g_row = b_last - b_row + i_row
        g_col = b_last - b_col + i_col
        m_new = jnp.maximum(b_last + m, jnp.max(g_row, axis=1, keepdims=True))
        decay = jnp.exp(b_last + m - m_new)
        wk = jnp.exp(g_col - m_new) * k
        c_ref[...] = decay * c + lax.dot_general(
            wk.astype(BF16), vb, (((0,), (0,)), ((), ())), preferred_element_type=F32)
        n_new = decay * n + jnp.sum(wk, axis=0, keepdims=True)
        return n_new, m_new

    n, m = lax.fori_loop(0, nc, body, (n0_ref[...], m0_ref[:, 0:1]))
    n_ref[...] = n
    m_ref[...] = jnp.broadcast_to(m, m_ref.shape)


def _mlstm(q, k, v, gi, gf, c0, n0, m0):
    B, T, _ = q.shape
    nc = T // CHUNK
    qkv_spec = pl.BlockSpec((None, T, M_DK), lambda b, h, d: (b, 0, h))
    gate_spec = pl.BlockSpec((None, None, None, nc, CHUNK), lambda b, h, d: (d, b, h, 0, 0))
    c_spec = pl.BlockSpec((None, None, None, M_DK, M_DK), lambda b, h, d: (b, d, h, 0, 0))
    n_spec = pl.BlockSpec((None, None, None, 1, M_DK), lambda b, h, d: (b, d, h, 0, 0))
    m_spec = pl.BlockSpec((None, None, None, 1, LANES), lambda b, h, d: (b, d, h, 0, 0))
    return pl.pallas_call(
        _mlstm_kernel,
        grid=(B, M_HEADS, N_DIR),
        in_specs=[qkv_spec, qkv_spec, qkv_spec, gate_spec, gate_spec, c_spec, n_spec, m_spec],
        out_specs=[pl.BlockSpec((None, None, T, M_DK), lambda b, h, d: (d, b, 0, h)),
                   c_spec, n_spec, m_spec],
        out_shape=[jax.ShapeDtypeStruct((N_DIR, B, T, M_WIDTH), F32),
                   jax.ShapeDtypeStruct((B, N_DIR, M_HEADS, M_DK, M_DK), F32),
                   jax.ShapeDtypeStruct((B, N_DIR, M_HEADS, 1, M_DK), F32),
                   jax.ShapeDtypeStruct((B, N_DIR, M_HEADS, 1, LANES), F32)],
        compiler_params=_params(3),
        name="mlstm_scan",
    )(q, k, v, gi, gf, c0, n0, m0)


N_SLOTS = 8
MIX_STEPS = 3 * HEAD_PAIRS
LORA_STEPS = 2 * N_DIR * HEAD_PAIRS
PREP_STEPS = MIX_STEPS + LORA_STEPS + HEAD_PAIRS


PREP_TT = 512


def _prep_kernel(x_ref, lo_ref, hi_ref, mu_ref, w_ref, o_ref, *, grid_shift, n_tok):
    tblk = pl.program_id(0)
    c = pl.program_id(1)
    x = x_ref[...]
    tt = x.shape[0]
    t = tblk * tt + lax.broadcasted_iota(jnp.int32, x.shape, 0)
    lane4 = lax.broadcasted_iota(jnp.int32, x.shape, 2) % 4

    def mixed():
        xe = jnp.concatenate([lo_ref[...], x, hi_ref[...]], axis=0)
        prev = xe[GRID_W - 1:GRID_W - 1 + tt]
        nxt = xe[GRID_W + 1:GRID_W + 1 + tt]
        if grid_shift:
            col = t % GRID_W
            left = jnp.where(col == 0, 0.0, prev)
            right = jnp.where(col == GRID_W - 1, 0.0, nxt)
            up = jnp.where(t < GRID_W, 0.0, xe[:tt])
            down = jnp.where(t >= n_tok - GRID_W, 0.0, xe[2 * GRID_W:])
            sh = jnp.where(lane4 == 0, left, jnp.where(lane4 == 1, right, jnp.where(lane4 == 2, up, down)))
        else:
            sh = jnp.where(lane4 % 2 == 0, jnp.where(t == 0, 0.0, prev), jnp.where(t == n_tok - 1, 0.0, nxt))
        return x + mu_ref[...] * (sh - x)

    def project(a):
        rows = a.reshape(tt * a.shape[1], LANES).astype(BF16)
        return jnp.dot(rows, w_ref[...].astype(BF16), preferred_element_type=F32).reshape(a.shape)

    @pl.when(c < MIX_STEPS)
    def _():
        o_ref[...] = mixed()

    @pl.when((c >= MIX_STEPS) & (c < MIX_STEPS + N_DIR * HEAD_PAIRS))
    def _():
        o_ref[...] = project(jnp.tanh(mixed()))

    @pl.when((c >= MIX_STEPS + N_DIR * HEAD_PAIRS) & (c < MIX_STEPS + LORA_STEPS))
    def _():
        o_ref[...] = project(mixed())

    @pl.when(c >= MIX_STEPS + LORA_STEPS)
    def _():
        o_ref[...] = _silu(x)


def _rwkv_prep(p, r_mu, w_lora, grid_shift):
    T, B, _ = p.shape
    tt = min(PREP_TT, T)
    halo_per_blk = tt // GRID_W
    n_halo = T // GRID_W
    lora_lo = MIX_STEPS
    rate_lo = MIX_STEPS + N_DIR * HEAD_PAIRS
    z_lo = MIX_STEPS + LORA_STEPS

    def src_tile(c):
        return jnp.where(c < lora_lo, c, jnp.where(c < rate_lo, MIX_STEPS, MIX_STEPS + 1))

    def col(c):
        return jnp.where(c < z_lo, RS_BLK0 + src_tile(c), RZ_BLK0 + c % HEAD_PAIRS)

    def w_map(t, c):
        return (jnp.clip((c - lora_lo) // HEAD_PAIRS, 0, 2 * N_DIR - 1), 0, c % HEAD_PAIRS)

    def o_map(t, c):
        slot = jnp.where(c < lora_lo, c // HEAD_PAIRS,
                         jnp.where(c < z_lo, 3 + (c - lora_lo) // HEAD_PAIRS, N_SLOTS - 1))
        return (slot, t, c % HEAD_PAIRS, 0, 0)

    return pl.pallas_call(
        functools.partial(_prep_kernel, grid_shift=grid_shift, n_tok=T),
        grid=(T // tt, PREP_STEPS),
        in_specs=[pl.BlockSpec((tt, B, LANES), lambda t, c: (t, 0, col(c))),
                  pl.BlockSpec((GRID_W, B, LANES),
                               lambda t, c: (jnp.maximum(t * halo_per_blk - 1, 0), 0, col(c))),
                  pl.BlockSpec((GRID_W, B, LANES),
                               lambda t, c: (jnp.minimum((t + 1) * halo_per_blk, n_halo - 1), 0, col(c))),
                  pl.BlockSpec((1, LANES), lambda t, c: (0, src_tile(c))),
                  pl.BlockSpec((None, LANES, LANES), w_map)],
        out_specs=pl.BlockSpec((None, tt, None, B, LANES), o_map),
        out_shape=jax.ShapeDtypeStruct((N_SLOTS, T, HEAD_PAIRS, B, LANES), F32),
        compiler_params=_params(2),
        name="rwkv_prep",
    )(p, p, p, r_mu, w_lora)


RWKV_TB = 16
J_UNROLL = 8
PREP_UNROLL = 4


def _chain_tiles(ref, t0, group, n_b):
    if n_b == SUBLANES:
        x = jnp.concatenate([ref[t0].reshape(R_N, LANES), ref[t0 + 1].reshape(R_N, LANES)], axis=0)
        xt = x.T
        top, bot = xt[:R_N], xt[R_N:]
        lo = lax.broadcasted_iota(jnp.int32, (R_N, LANES), 1) < R_N
        return [jnp.where(lo, top, pltpu.roll(bot, R_N, 1)), jnp.where(lo, pltpu.roll(top, R_N, 1), bot)]
    xt = ref[t0].reshape(LANES, LANES).T
    return [jnp.where(group == 0, xt[:R_N], xt[R_N:])]


def _rwkv_kernel(r_ref, k_ref, v_ref, lw_ref, la_ref, pd_ref, ps_ref, s0_ref,
                 y_ref, c_ref, s_ref, kk_s, rr_s, vv_s, dd_s, kz_s, bb_s, *, n_b):
    dirn = pl.program_id(0)
    group = pl.program_id(1)
    tb = pl.program_id(2)
    n_tok = r_ref.shape[0]
    per_xpose = 2 if n_b == SUBLANES else 1

    @pl.when(tb == 0)
    def _():
        s_ref[...] = s0_ref[...]

    w0, a0 = pd_ref[0], pd_ref[1]
    k_k, k_a, r_k = ps_ref[0], ps_ref[1], ps_ref[2]

    def prep(tp, carry):
        t0 = tp * per_xpose
        tiles = [_chain_tiles(ref, t0, group, n_b) for ref in (r_ref, k_ref, v_ref, lw_ref, la_ref)]
        for i in range(per_xpose):
            r, kraw, v, lw, la = (q[i] for q in tiles)
            t = t0 + i
            wlog = -_softplus(-(w0 + lw)) - 0.5
            a = _sigmoid(a0 + la)
            kk = kraw * k_k
            kk = kk / jnp.maximum(jnp.sqrt(jnp.sum(kk * kk, axis=0, keepdims=True)), 1e-12)
            kz = kraw * (1.0 + (a - 1.0) * k_a)
            rr_s[t] = r
            vv_s[t] = v
            kk_s[t] = kk
            dd_s[t] = jnp.exp(-jnp.exp(wlog))
            kz_s[t] = kz
            bb_s[t] = kk * a
            c_ref[t] = jnp.sum(r * kz * r_k, axis=0, keepdims=True) * v
        return carry

    lax.fori_loop(0, n_tok // per_xpose, prep, 0, unroll=PREP_UNROLL)

    def step(t, carry):
        tt = t + dirn * (n_tok - 1 - 2 * t)

        def dot_kk(jb, u):
            for jj in range(J_UNROLL):
                j = jb * J_UNROLL + jj
                u = u + s_ref[j] * kk_s[tt, pl.ds(j, 1), :]
            return u

        u = lax.fori_loop(0, R_N // J_UNROLL, dot_kk, jnp.zeros((R_N, LANES), F32))
        sa = -u
        vt = vv_s[tt]

        def update(jb, y):
            for jj in range(J_UNROLL):
                j = jb * J_UNROLL + jj
                sn = (s_ref[j] * dd_s[tt, pl.ds(j, 1), :] + sa * bb_s[tt, pl.ds(j, 1), :]
                      + vt * kz_s[tt, pl.ds(j, 1), :])
                s_ref[j] = sn
                y = y + sn * rr_s[tt, pl.ds(j, 1), :]
            return y

        y_ref[tt] = lax.fori_loop(0, R_N // J_UNROLL, update, jnp.zeros((R_N, LANES), F32))
        return carry

    lax.fori_loop(0, n_tok, step, 0)


def _rwkv(xs, p_dir, p_sh, s0):
    _, T, _, B, _ = xs.shape
    G = p_sh.shape[0]
    tb = min(RWKV_TB, T)
    nt = T // tb
    rev = lambda dd, t: t + dd * (nt - 1 - 2 * t)

    def slot(s):
        return pl.BlockSpec((None, tb, HEAD_PAIRS, B, LANES),
                            lambda dd, g, t: (s(dd), rev(dd, t), 0, 0, 0))

    state = pl.BlockSpec((None, None, R_N, R_N, LANES), lambda dd, g, t: (dd, g, 0, 0, 0))
    out = pl.BlockSpec((None, tb, R_N, LANES), lambda dd, g, t: (dd, rev(dd, t), 0, g))
    scratch = pltpu.VMEM((tb, R_N, LANES), F32)
    return pl.pallas_call(
        functools.partial(_rwkv_kernel, n_b=B),
        grid=(N_DIR, G, nt),
        in_specs=[slot(lambda dd: 0), slot(lambda dd: 1), slot(lambda dd: 2),
                  slot(lambda dd: 3 + dd), slot(lambda dd: 5 + dd),
                  pl.BlockSpec((None, None, 2, R_N, LANES), lambda dd, g, t: (dd, g, 0, 0, 0)),
                  pl.BlockSpec((None, 3, R_N, LANES), lambda dd, g, t: (g, 0, 0, 0)),
                  state],
        out_specs=[out, out, state],
        out_shape=[jax.ShapeDtypeStruct((N_DIR, T, R_N, G * LANES), F32),
                   jax.ShapeDtypeStruct((N_DIR, T, R_N, G * LANES), F32),
                   jax.ShapeDtypeStruct((N_DIR, G, R_N, R_N, LANES), F32)],
        scratch_shapes=[scratch] * 6,
        compiler_params=_params(3),
        name="rwkv_scan",
    )(xs, xs, xs, xs, xs, p_dir, p_sh, s0)


def _post_kernel(y_ref, c_ref, z_ref, ln_ref, o_ref, *, n_b):
    n_tok = y_ref.shape[1]
    per_xpose = 2 if n_b == SUBLANES else 1
    lo = lax.broadcasted_iota(jnp.int32, (R_N, LANES), 1) < R_N

    def normed(t, g):
        lanes = pl.ds(g * LANES, LANES)
        y = y_ref[0, t, :, lanes] + y_ref[1, t, :, lanes]
        mean = jnp.mean(y, axis=0, keepdims=True)
        yc = y - mean
        var = jnp.mean(yc * yc, axis=0, keepdims=True)
        return (yc * lax.rsqrt(var + LNX_EPS) * ln_ref[0, :, lanes] + ln_ref[1, :, lanes]
                + c_ref[0, t, :, lanes] + c_ref[1, t, :, lanes])

    def body(tp, carry):
        t0 = tp * per_xpose
        if n_b == SUBLANES:
            a, b = normed(t0, 0), normed(t0 + 1, 0)
            xt = jnp.concatenate([jnp.where(lo, a, pltpu.roll(b, R_N, 1)),
                                  jnp.where(lo, pltpu.roll(a, R_N, 1), b)], axis=0)
            x = xt.T
            for i in range(2):
                rows = x[i * R_N:(i + 1) * R_N].reshape(HEAD_PAIRS, n_b, LANES)
                o_ref[t0 + i] = rows * z_ref[t0 + i]
        else:
            x = jnp.concatenate([normed(t0, 0), normed(t0, 1)], axis=0).T
            o_ref[t0] = x.reshape(HEAD_PAIRS, n_b, LANES) * z_ref[t0]
        return carry

    lax.fori_loop(0, n_tok // per_xpose, body, 0)


def _rwkv_post(y, c, xs, ln):
    _, T, _, CH = y.shape
    B = xs.shape[3]
    tb = min(RWKV_TB, T)
    chain = pl.BlockSpec((N_DIR, tb, R_N, CH), lambda t: (0, t, 0, 0))
    nat = pl.BlockSpec((None, tb, HEAD_PAIRS, B, LANES), lambda t: (N_SLOTS - 1, t, 0, 0, 0))
    return pl.pallas_call(
        functools.partial(_post_kernel, n_b=B),
        grid=(T // tb,),
        in_specs=[chain, chain, nat, pl.BlockSpec((2, R_N, CH), lambda t: (0, 0, 0))],
        out_specs=pl.BlockSpec((tb, HEAD_PAIRS, B, LANES), lambda t: (t, 0, 0, 0)),
        out_shape=jax.ShapeDtypeStruct((T, HEAD_PAIRS, B, LANES), F32),
        compiler_params=_params(1),
        name="rwkv_post",
    )(y, c, xs, ln)


OUT_TM = 512


def _out_kernel(hm_ref, yr_ref, w_ref, x_ref, gate_ref, g_ref, o_ref):
    cat = jnp.concatenate([hm_ref[...], yr_ref[...].astype(BF16)], axis=1)
    out = jnp.dot(cat, w_ref[...], preferred_element_type=F32)
    z = x_ref[...] + gate_ref[...] * out
    o_ref[...] = z * lax.rsqrt(jnp.mean(z * z, axis=-1, keepdims=True) + EPS) * g_ref[...]


def _out_proj(hm, yr, w_out, x, gate, final_g):
    B, T, D = x.shape
    tm = min(OUT_TM, T)
    per_b = gate.shape[0] == B
    row = lambda b, t: (b, t, 0)
    return pl.pallas_call(
        _out_kernel,
        grid=(B, T // tm),
        in_specs=[pl.BlockSpec((None, tm, M_WIDTH), row),
                  pl.BlockSpec((None, tm, R_WIDTH), row),
                  pl.BlockSpec((D, D), lambda b, t: (0, 0)),
                  pl.BlockSpec((None, tm, D), row),
                  pl.BlockSpec((None, 1, D), lambda b, t: (b if per_b else 0, 0, 0)),
                  pl.BlockSpec((1, D), lambda b, t: (0, 0))],
        out_specs=pl.BlockSpec((None, tm, D), row),
        out_shape=jax.ShapeDtypeStruct((B, T, D), F32),
        compiler_params=_params(2),
        name="out_proj",
    )(hm, yr, w_out, x, gate, final_g)


def _rmsnorm(x, g):
    return x * lax.rsqrt(jnp.mean(x * x, axis=-1, keepdims=True) + EPS) * g


def _centred_conv(p, w, b):
    pad = CONV_K // 2
    T = p.shape[1]
    pp = jnp.pad(p, ((0, 0), (pad, pad), (0, 0)))
    return sum(pp[:, j:j + T] * w[j] for j in range(CONV_K)) + b


def _chain_param(p, n_b):
    q = p.reshape(HEAD_PAIRS, 2, R_N)
    if n_b == SUBLANES:
        t = jnp.broadcast_to(q.transpose(2, 1, 0)[..., None], (R_N, 2, HEAD_PAIRS, n_b))
        return t.reshape(1, R_N, LANES)
    t = jnp.broadcast_to(q.transpose(1, 2, 0)[..., None], (2, R_N, HEAD_PAIRS, n_b))
    return t.reshape(2, R_N, LANES)


def _chain_state(s, n_b):
    s6 = s.reshape(n_b, N_DIR, HEAD_PAIRS, 2, R_N, R_N)
    if n_b == SUBLANES:
        return s6.transpose(1, 5, 4, 3, 2, 0).reshape(N_DIR, 1, R_N, R_N, LANES)
    return s6.transpose(1, 3, 5, 4, 2, 0).reshape(N_DIR, 2, R_N, R_N, LANES)


def _unchain_state(s, n_b):
    if n_b == SUBLANES:
        s6 = s.reshape(N_DIR, R_N, R_N, 2, HEAD_PAIRS, n_b).transpose(5, 0, 4, 3, 2, 1)
    else:
        s6 = s.reshape(N_DIR, 2, R_N, R_N, HEAD_PAIRS, n_b).transpose(5, 0, 4, 1, 3, 2)
    return s6.reshape(n_b, N_DIR, R_HEADS, R_N, R_N)


def _block(x, mod, st, lw, norm_g, final_g, grid):
    (w_mproj, w_rproj, m_conv_w, m_conv_b, m_gate_b, m_ln_g, r_mu, r_w0, w_lora, r_a0,
     r_k_k, r_k_a, r_r_k, r_ln_g, r_ln_b, w_out) = lw
    C0, n0, m0, S0 = st
    B, T, D = x.shape
    assert B in (SUBLANES, 2 * SUBLANES), "chain layout is written for 8 or 16 sequences"
    shift, scale, gate = jnp.split(mod, 3, axis=-1)
    h = (_rmsnorm(x, norm_g) * (1.0 + scale) + shift).astype(BF16)
    tm = min(512, B * T)
    p = _mm(h.reshape(B * T, D), w_mproj, tm, M_PROJ_COLS // 2).reshape(B, T, M_PROJ_COLS)
    pr = _mm(h.transpose(1, 0, 2).reshape(T * B, D), w_rproj, tm, R_PROJ_COLS // 2)
    pr = pr.reshape(T, B, R_PROJ_COLS)
    mq, mk, mv, mo, mz = (p[..., i * M_WIDTH:(i + 1) * M_WIDTH] for i in range(5))
    mg = p[..., 5 * M_WIDTH:5 * M_WIDTH + M_GATE_COLS]

    qk = jax.nn.silu(_centred_conv(jnp.concatenate([mq, mk], -1), m_conv_w, m_conv_b))
    q = qk[..., :M_WIDTH]
    k = qk[..., M_WIDTH:] * (M_DK ** -0.5)
    gates = (mg.reshape(B, T, N_DIR, 2, M_HEADS) + m_gate_b).transpose(2, 3, 0, 4, 1)
    nc = T // CHUNK
    log_i = gates[:, 0].reshape(N_DIR, B, M_HEADS, nc, CHUNK)
    log_f = jax.nn.log_sigmoid(gates[:, 1]).reshape(N_DIR, B, M_HEADS, nc, CHUNK)
    h_dirs, Cn, nn, mn = _mlstm(
        q, k, mv, log_i, log_f, C0, n0[..., None, :],
        jnp.broadcast_to(m0[..., None, None], m0.shape + (1, LANES)))
    hm = jax.nn.sigmoid(mo) * (h_dirs[0] + h_dirs[1])
    hm = hm.reshape(B, T, M_HEADS, M_DK)
    mu = hm.mean(-1, keepdims=True)
    hm = (hm - mu) * lax.rsqrt(jnp.mean((hm - mu) ** 2, axis=-1, keepdims=True) + EPS)
    hm = (hm.reshape(B, T, M_WIDTH) * m_ln_g * jax.nn.silu(mz)).astype(BF16)

    xs = _rwkv_prep(pr, r_mu.reshape(1, SHIFT_COLS), w_lora, grid)
    p_dir = jnp.stack([jnp.stack([_chain_param(r_w0[z], B), _chain_param(r_a0[z], B)], axis=1)
                       for z in range(N_DIR)])
    p_sh = jnp.stack([_chain_param(r_k_k, B), _chain_param(r_k_a, B),
                      _chain_param(r_r_k.reshape(R_WIDTH), B)], axis=1)
    y_dirs, c_dirs, S_new = _rwkv(xs, p_dir, p_sh, _chain_state(S0, B))
    G = p_sh.shape[0]
    ln = jnp.stack([_chain_param(r_ln_g, B), _chain_param(r_ln_b, B)])
    ln = ln.transpose(0, 2, 1, 3).reshape(2, R_N, G * LANES)
    yr = _rwkv_post(y_dirs, c_dirs, xs, ln)
    yr = yr.transpose(2, 0, 1, 3).reshape(B, T, R_WIDTH)

    out = _out_proj(hm, yr, w_out, x, gate, final_g.reshape(1, D))
    return out, (Cn, nn[..., 0, :], mn[..., 0, 0], _unchain_state(S_new, B))


def kernel(x_prompt, x_sample, state_mlstm_C, state_mlstm_n, state_mlstm_m, state_rwkv_S, c, c_ctx,
           norm_g, w_ada, b_ada, w_in, m_conv_w, m_conv_b, m_gate_b, m_ln_g, r_mu, r_w0, r_w2, r_a0,
           r_a2, r_k_k, r_k_a, r_r_k, r_ln_g, r_ln_b, w_out, final_g):
    depth = w_in.shape[0]
    assert depth == 1, "the final RMSNorm is fused into the single layer's output projection"
    bp = x_prompt.shape[0]
    ctx_state0 = (jnp.zeros((bp, N_DIR, M_HEADS, M_DK, M_DK), F32),
                  jnp.zeros((bp, N_DIR, M_HEADS, M_DK), F32),
                  jnp.full((bp, N_DIR, M_HEADS), -jnp.inf, F32),
                  jnp.zeros((bp, N_DIR, R_HEADS, R_N, R_N), F32))
    l = 0
    w_l = w_in[l]
    gate_lo = 5 * M_WIDTH
    gate_hi = gate_lo + M_GATE_COLS
    w_mproj = jnp.concatenate(
        [w_l[:, :gate_hi], jnp.zeros((D_MODEL, GATE_PAD - M_GATE_COLS), F32)], axis=1).astype(BF16)
    w_rproj = w_l[:, gate_hi:].astype(BF16)
    zpad = jnp.zeros((LORA, R_WIDTH), F32)
    w_lora = jnp.stack([jnp.concatenate([r_w2[l, 0], zpad]), jnp.concatenate([zpad, r_w2[l, 1]]),
                        jnp.concatenate([r_a2[l, 0], zpad]), jnp.concatenate([zpad, r_a2[l, 1]])])
    lw = (w_mproj, w_rproj, m_conv_w[l], m_conv_b[l], m_gate_b[l], m_ln_g[l], r_mu[l], r_w0[l], w_lora,
          r_a0[l], r_k_k[l], r_k_a[l], r_r_k[l], r_ln_g[l], r_ln_b[l], w_out[l].astype(BF16))
    n_cond = 1 + c.shape[0]
    cond = jnp.concatenate([c_ctx[None, :], c, jnp.zeros((16 - n_cond, D_MODEL), F32)], axis=0)
    mod = _mm(jax.nn.silu(cond), w_ada[l], 16, 1536)[:n_cond] + b_ada[l]
    mod_p = mod[0:1, None, :]
    mod_s = mod[1:, None, :]
    y_prompt, (Cp, np_, mp, Sp) = _block(x_prompt, mod_p, ctx_state0, lw, norm_g[l], final_g, False)
    st_s = (state_mlstm_C[:, l], state_mlstm_n[:, l], state_mlstm_m[:, l], state_rwkv_S[:, l])
    y_sample, _ = _block(x_sample, mod_s, st_s, lw, norm_g[l], final_g, True)
    return (y_prompt, y_sample, Cp[:, None], np_[:, None], mp[:, None], Sp[:, None])
```

```python
import functools

import jax
import jax.numpy as jnp
import numpy as np
from jax import lax
from jax.experimental import pallas as pl
from jax.experimental.pallas import tpu as pltpu

D_MODEL = 2048
GRID_W = 64
N_DIR = 2
M_WIDTH = D_MODEL // 2
M_HEADS = 4
M_DK = M_WIDTH // M_HEADS
R_WIDTH = D_MODEL - M_WIDTH
R_N = 64
R_HEADS = R_WIDTH // R_N
LORA = 64
CONV_K = 3
CHUNK = 64
EPS = 1e-6
LNX_EPS = 64e-5
M_GATE_COLS = N_DIR * 2 * M_HEADS
SHIFT_COLS = 3 * R_WIDTH + 2 * N_DIR * LORA

LANES = 128
SUBLANES = 8
HEAD_PAIRS = R_WIDTH // LANES
GATE_PAD = 2 * LANES
M_PROJ_COLS = 5 * M_WIDTH + GATE_PAD
R_PROJ_COLS = R_WIDTH + SHIFT_COLS
RZ_BLK0 = 0
RS_BLK0 = R_WIDTH // LANES
VMEM_LIMIT = 48 * 1024 * 1024

F32 = jnp.float32
BF16 = jnp.bfloat16


def _params(n_axes):
    return pltpu.CompilerParams(dimension_semantics=("arbitrary",) * n_axes,
                                vmem_limit_bytes=VMEM_LIMIT)


def _sigmoid(x):
    return 1.0 / (1.0 + jnp.exp(-x))


def _silu(x):
    return x * _sigmoid(x)


def _softplus(x):
    return jnp.maximum(x, 0.0) + jnp.log(1.0 + jnp.exp(-jnp.abs(x)))


def _mm_kernel(x_ref, w_ref, o_ref):
    o_ref[...] = jnp.dot(x_ref[...].astype(BF16), w_ref[...].astype(BF16),
                         preferred_element_type=F32)


def _mm(x, w, tm, tn):
    M, K = x.shape
    N = w.shape[1]
    assert M % tm == 0 and N % tn == 0
    return pl.pallas_call(
        _mm_kernel,
        grid=(N // tn, M // tm),
        in_specs=[pl.BlockSpec((tm, K), lambda j, i: (i, 0)),
                  pl.BlockSpec((K, tn), lambda j, i: (0, j))],
        out_specs=pl.BlockSpec((tm, tn), lambda j, i: (i, j)),
        out_shape=jax.ShapeDtypeStruct((M, N), F32),
        compiler_params=_params(2),
        name="proj_mm",
    )(x, w)


MLSTM_UNROLL = 2


def _mlstm_kernel(q_ref, k_ref, v_ref, o_ref, z_ref, wq_ref, wk_ref, bq_ref, bk_ref, lng_ref,
                  gi_ref, gf_ref, c0_ref, n0_ref, m0_ref,
                  hm_ref, c_ref, n_ref, m_ref, qs, ks, hs_f, hs_b):
    n_tok = q_ref.shape[0]
    nc = n_tok // CHUNK
    L = CHUNK
    tpos = lax.broadcasted_iota(jnp.int32, (n_tok, M_DK), 0)

    def conv_silu(x_ref, w_ref, b_ref):
        x = x_ref[...]
        prev = jnp.where(tpos == 0, 0.0, pltpu.roll(x, 1, 0))
        nxt = jnp.where(tpos == n_tok - 1, 0.0, pltpu.roll(x, n_tok - 1, 0))
        return _silu(prev * w_ref[0:1, :] + x * w_ref[1:2, :] + nxt * w_ref[2:3, :] + b_ref[...])

    qs[...] = conv_silu(q_ref, wq_ref, bq_ref)
    ks[...] = conv_silu(k_ref, wk_ref, bk_ref) * (M_DK ** -0.5)
    c_ref[...] = c0_ref[...]

    row = lax.broadcasted_iota(jnp.int32, (L, L), 0)
    col = lax.broadcasted_iota(jnp.int32, (L, L), 1)
    eye = row == col
    neg_inf = jnp.float32(-jnp.inf)

    def chunk(dirn, cc, n, m):
        seen = col <= row if dirn == 0 else col >= row
        seen_t = row <= col if dirn == 0 else row >= col
        r0 = pl.multiple_of(cc * L, L)
        q = qs[pl.ds(r0, L), :]
        k = ks[pl.ds(r0, L), :]
        v = v_ref[pl.ds(r0, L), :]
        f_row = gf_ref[dirn, pl.ds(cc, 1), :]
        i_row = gi_ref[dirn, pl.ds(cc, 1), :]
        f_col = jnp.sum(jnp.where(eye, f_row, 0.0), axis=1, keepdims=True)
        i_col = jnp.sum(jnp.where(eye, i_row, 0.0), axis=1, keepdims=True)
        b_col = jnp.sum(jnp.where(seen, f_row, 0.0), axis=1, keepdims=True)
        b_row = jnp.sum(jnp.where(seen_t, f_col, 0.0), axis=0, keepdims=True)
        dmat = jnp.where(seen, b_col - b_row + i_row, neg_inf)
        inter = b_col + m
        mt = jnp.maximum(inter, jnp.max(dmat, axis=1, keepdims=True))
        qb = q.astype(BF16)
        kb = k.astype(BF16)
        vb = v.astype(BF16)
        qk = lax.dot_general(qb, kb, (((1,), (1,)), ((), ())), preferred_element_type=F32)
        a = jnp.exp(dmat - mt) * qk
        s_in = jnp.exp(inter - mt)
        c = c_ref[dirn]
        qc = jnp.dot(qb, c.astype(BF16), preferred_element_type=F32)
        av = jnp.dot(a.astype(BF16), vb, preferred_element_type=F32)
        num = s_in * qc + av
        den = s_in * jnp.sum(q * n, axis=1, keepdims=True) + jnp.sum(a, axis=1, keepdims=True)
        hs = hs_f if dirn == 0 else hs_b
        hs[pl.ds(r0, L), :] = num / jnp.maximum(jnp.abs(den), jnp.exp(-mt))
        b_last = jnp.sum(f_row, axis=1, keepdims=True)
        g_row = b_last - b_row + i_row
        g_col = b_last - b_col + i_col
        m_new = jnp.maximum(b_last + m, jnp.max(g_row, axis=1, keepdims=True))
        decay = jnp.exp(b_last + m - m_new)
        wk = jnp.exp(g_col - m_new) * k
        c_ref[dirn] = decay * c + lax.dot_general(
            wk.astype(BF16), vb, (((0,), (0,)), ((), ())), preferred_element_type=F32)
        n_new = decay * n + jnp.sum(wk, axis=0, keepdims=True)
        return n_new, m_new

    def body(ci, carry):
        n_f, m_f, n_b, m_b = carry
        n_f, m_f = chunk(0, ci, n_f, m_f)
        n_b, m_b = chunk(1, nc - 1 - ci, n_b, m_b)
        return n_f, m_f, n_b, m_b

    n_f, m_f, n_b, m_b = lax.fori_loop(
        0, nc, body, (n0_ref[0], m0_ref[0, :, 0:1], n0_ref[1], m0_ref[1, :, 0:1]),
        unroll=MLSTM_UNROLL)
    n_ref[0] = n_f
    n_ref[1] = n_b
    m_ref[0] = jnp.broadcast_to(m_f, (1, LANES))
    m_ref[1] = jnp.broadcast_to(m_b, (1, LANES))

    hm = _sigmoid(o_ref[...]) * (hs_f[...] + hs_b[...])
    hc = hm - jnp.mean(hm, axis=1, keepdims=True)
    hn = hc * lax.rsqrt(jnp.mean(hc * hc, axis=1, keepdims=True) + EPS)
    hm_ref[...] = (hn * lng_ref[...] * _silu(z_ref[...])).astype(BF16)


def _mlstm(p, conv_w, conv_b, ln_g, gi, gf, c0, n0, m0):
    B, T, _ = p.shape
    nc = T // CHUNK
    tiles_per_part = M_WIDTH // M_DK

    def col_spec(part):
        return pl.BlockSpec((None, T, M_DK), lambda b, h: (b, 0, part * tiles_per_part + h))

    def par_spec(rows, part):
        return pl.BlockSpec((rows, M_DK), lambda b, h: (0, part * tiles_per_part + h))

    gate_spec = pl.BlockSpec((N_DIR, None, None, nc, CHUNK), lambda b, h: (0, b, h, 0, 0))
    c_spec = pl.BlockSpec((None, N_DIR, None, M_DK, M_DK), lambda b, h: (b, 0, h, 0, 0))
    n_spec = pl.BlockSpec((None, N_DIR, None, 1, M_DK), lambda b, h: (b, 0, h, 0, 0))
    m_spec = pl.BlockSpec((None, N_DIR, None, 1, LANES), lambda b, h: (b, 0, h, 0, 0))
    seq = pltpu.VMEM((T, M_DK), F32)
    return pl.pallas_call(
        _mlstm_kernel,
        grid=(B, M_HEADS),
        in_specs=[col_spec(0), col_spec(1), col_spec(2), col_spec(3), col_spec(4),
                  par_spec(CONV_K, 0), par_spec(CONV_K, 1), par_spec(1, 0), par_spec(1, 1),
                  par_spec(1, 0), gate_spec, gate_spec, c_spec, n_spec, m_spec],
        out_specs=[pl.BlockSpec((None, T, M_DK), lambda b, h: (b, 0, h)), c_spec, n_spec, m_spec],
        out_shape=[jax.ShapeDtypeStruct((B, T, M_WIDTH), BF16),
                   jax.ShapeDtypeStruct((B, N_DIR, M_HEADS, M_DK, M_DK), F32),
                   jax.ShapeDtypeStruct((B, N_DIR, M_HEADS, 1, M_DK), F32),
                   jax.ShapeDtypeStruct((B, N_DIR, M_HEADS, 1, LANES), F32)],
        scratch_shapes=[seq, seq, seq, seq],
        compiler_params=_params(2),
        name="mlstm_scan",
    )(p, p, p, p, p, conv_w, conv_w, conv_b, conv_b, ln_g, gi, gf, c0, n0, m0)


N_SLOTS = 8
MIX_STEPS = 3 * HEAD_PAIRS
LORA_STEPS = 2 * N_DIR * HEAD_PAIRS
PREP_STEPS = MIX_STEPS + LORA_STEPS + HEAD_PAIRS


PREP_TT = 512


def _prep_kernel(x_ref, lo_ref, hi_ref, mu_ref, w_ref, o_ref, *, grid_shift, n_tok):
    tblk = pl.program_id(0)
    c = pl.program_id(1)
    x = x_ref[...]
    tt = x.shape[0]
    t = tblk * tt + lax.broadcasted_iota(jnp.int32, x.shape, 0)
    lane4 = lax.broadcasted_iota(jnp.int32, x.shape, 2) % 4

    def mixed():
        xe = jnp.concatenate([lo_ref[...], x, hi_ref[...]], axis=0)
        prev = xe[GRID_W - 1:GRID_W - 1 + tt]
        nxt = xe[GRID_W + 1:GRID_W + 1 + tt]
        if grid_shift:
            col = t % GRID_W
            left = jnp.where(col == 0, 0.0, prev)
            right = jnp.where(col == GRID_W - 1, 0.0, nxt)
            up = jnp.where(t < GRID_W, 0.0, xe[:tt])
            down = jnp.where(t >= n_tok - GRID_W, 0.0, xe[2 * GRID_W:])
            sh = jnp.where(lane4 == 0, left, jnp.where(lane4 == 1, right, jnp.where(lane4 == 2, up, down)))
        else:
            sh = jnp.where(lane4 % 2 == 0, jnp.where(t == 0, 0.0, prev), jnp.where(t == n_tok - 1, 0.0, nxt))
        return x + mu_ref[...] * (sh - x)

    def project(a):
        rows = a.reshape(tt * a.shape[1], LANES).astype(BF16)
        return jnp.dot(rows, w_ref[...].astype(BF16), preferred_element_type=F32).reshape(a.shape)

    @pl.when(c < MIX_STEPS)
    def _():
        o_ref[...] = mixed()

    @pl.when((c >= MIX_STEPS) & (c < MIX_STEPS + N_DIR * HEAD_PAIRS))
    def _():
        o_ref[...] = project(jnp.tanh(mixed()))

    @pl.when((c >= MIX_STEPS + N_DIR * HEAD_PAIRS) & (c < MIX_STEPS + LORA_STEPS))
    def _():
        o_ref[...] = project(mixed())

    @pl.when(c >= MIX_STEPS + LORA_STEPS)
    def _():
        o_ref[...] = _silu(x)


def _rwkv_prep(p, r_mu, w_lora, grid_shift):
    T, B, _ = p.shape
    tt = min(PREP_TT, T)
    halo_per_blk = tt // GRID_W
    n_halo = T // GRID_W
    lora_lo = MIX_STEPS
    rate_lo = MIX_STEPS + N_DIR * HEAD_PAIRS
    z_lo = MIX_STEPS + LORA_STEPS

    def src_tile(c):
        return jnp.where(c < lora_lo, c, jnp.where(c < rate_lo, MIX_STEPS, MIX_STEPS + 1))

    def col(c):
        return jnp.where(c < z_lo, RS_BLK0 + src_tile(c), RZ_BLK0 + c % HEAD_PAIRS)

    def w_map(t, c):
        return (jnp.clip((c - lora_lo) // HEAD_PAIRS, 0, 2 * N_DIR - 1), 0, c % HEAD_PAIRS)

    def o_map(t, c):
        slot = jnp.where(c < lora_lo, c // HEAD_PAIRS,
                         jnp.where(c < z_lo, 3 + (c - lora_lo) // HEAD_PAIRS, N_SLOTS - 1))
        return (slot, t, c % HEAD_PAIRS, 0, 0)

    return pl.pallas_call(
        functools.partial(_prep_kernel, grid_shift=grid_shift, n_tok=T),
        grid=(T // tt, PREP_STEPS),
        in_specs=[pl.BlockSpec((tt, B, LANES), lambda t, c: (t, 0, col(c))),
                  pl.BlockSpec((GRID_W, B, LANES),
                               lambda t, c: (jnp.maximum(t * halo_per_blk - 1, 0), 0, col(c))),
                  pl.BlockSpec((GRID_W, B, LANES),
                               lambda t, c: (jnp.minimum((t + 1) * halo_per_blk, n_halo - 1), 0, col(c))),
                  pl.BlockSpec((1, LANES), lambda t, c: (0, src_tile(c))),
                  pl.BlockSpec((None, LANES, LANES), w_map)],
        out_specs=pl.BlockSpec((None, tt, None, B, LANES), o_map),
        out_shape=jax.ShapeDtypeStruct((N_SLOTS, T, HEAD_PAIRS, B, LANES), F32),
        compiler_params=_params(2),
        name="rwkv_prep",
    )(p, p, p, r_mu, w_lora)


RWKV_TB = 16
J_UNROLL = 8
PREP_UNROLL = 4


def _chain_tiles(ref, t0, group, n_b):
    if n_b == SUBLANES:
        x = jnp.concatenate([ref[t0].reshape(R_N, LANES), ref[t0 + 1].reshape(R_N, LANES)], axis=0)
        xt = x.T
        top, bot = xt[:R_N], xt[R_N:]
        lo = lax.broadcasted_iota(jnp.int32, (R_N, LANES), 1) < R_N
        return [jnp.where(lo, top, pltpu.roll(bot, R_N, 1)), jnp.where(lo, pltpu.roll(top, R_N, 1), bot)]
    xt = ref[t0].reshape(LANES, LANES).T
    return [jnp.where(group == 0, xt[:R_N], xt[R_N:])]


def _rwkv_kernel(r_ref, k_ref, v_ref, lw_ref, la_ref, pd_ref, ps_ref, s0_ref,
                 y_ref, c_ref, s_ref, kk_s, rr_s, vv_s, dd_s, kz_s, bb_s, *, n_b):
    dirn = pl.program_id(0)
    group = pl.program_id(1)
    tb = pl.program_id(2)
    n_tok = r_ref.shape[0]
    per_xpose = 2 if n_b == SUBLANES else 1

    @pl.when(tb == 0)
    def _():
        s_ref[...] = s0_ref[...]

    w0, a0 = pd_ref[0], pd_ref[1]
    k_k, k_a, r_k = ps_ref[0], ps_ref[1], ps_ref[2]

    def prep(tp, carry):
        t0 = tp * per_xpose
        tiles = [_chain_tiles(ref, t0, group, n_b) for ref in (r_ref, k_ref, v_ref, lw_ref, la_ref)]
        for i in range(per_xpose):
            r, kraw, v, lw, la = (q[i] for q in tiles)
            t = t0 + i
            wlog = -_softplus(-(w0 + lw)) - 0.5
            a = _sigmoid(a0 + la)
            kk = kraw * k_k
            kk = kk / jnp.maximum(jnp.sqrt(jnp.sum(kk * kk, axis=0, keepdims=True)), 1e-12)
            kz = kraw * (1.0 + (a - 1.0) * k_a)
            rr_s[t] = r
            vv_s[t] = v
            kk_s[t] = kk
            dd_s[t] = jnp.exp(-jnp.exp(wlog))
            kz_s[t] = kz
            bb_s[t] = kk * a
            c_ref[t] = jnp.sum(r * kz * r_k, axis=0, keepdims=True) * v
        return carry

    lax.fori_loop(0, n_tok // per_xpose, prep, 0, unroll=PREP_UNROLL)

    def step(t, carry):
        tt = t + dirn * (n_tok - 1 - 2 * t)

        def dot_kk(jb, u):
            for jj in range(J_UNROLL):
                j = jb * J_UNROLL + jj
                u = u + s_ref[j] * kk_s[tt, pl.ds(j, 1), :]
            return u

        u = lax.fori_loop(0, R_N // J_UNROLL, dot_kk, jnp.zeros((R_N, LANES), F32))
        sa = -u
        vt = vv_s[tt]

        def update(jb, y):
            for jj in range(J_UNROLL):
                j = jb * J_UNROLL + jj
                sn = (s_ref[j] * dd_s[tt, pl.ds(j, 1), :] + sa * bb_s[tt, pl.ds(j, 1), :]
                      + vt * kz_s[tt, pl.ds(j, 1), :])
                s_ref[j] = sn
                y = y + sn * rr_s[tt, pl.ds(j, 1), :]
            return y

        y_ref[tt] = lax.fori_loop(0, R_N // J_UNROLL, update, jnp.zeros((R_N, LANES), F32))
        return carry

    lax.fori_loop(0, n_tok, step, 0)


def _rwkv(xs, p_dir, p_sh, s0):
    _, T, _, B, _ = xs.shape
    G = p_sh.shape[0]
    tb = min(RWKV_TB, T)
    nt = T // tb
    rev = lambda dd, t: t + dd * (nt - 1 - 2 * t)

    def slot(s):
        return pl.BlockSpec((None, tb, HEAD_PAIRS, B, LANES),
                            lambda dd, g, t: (s(dd), rev(dd, t), 0, 0, 0))

    state = pl.BlockSpec((None, None, R_N, R_N, LANES), lambda dd, g, t: (dd, g, 0, 0, 0))
    out = pl.BlockSpec((None, tb, R_N, LANES), lambda dd, g, t: (dd, rev(dd, t), 0, g))
    scratch = pltpu.VMEM((tb, R_N, LANES), F32)
    return pl.pallas_call(
        functools.partial(_rwkv_kernel, n_b=B),
        grid=(N_DIR, G, nt),
        in_specs=[slot(lambda dd: 0), slot(lambda dd: 1), slot(lambda dd: 2),
                  slot(lambda dd: 3 + dd), slot(lambda dd: 5 + dd),
                  pl.BlockSpec((None, None, 2, R_N, LANES), lambda dd, g, t: (dd, g, 0, 0, 0)),
                  pl.BlockSpec((None, 3, R_N, LANES), lambda dd, g, t: (g, 0, 0, 0)),
                  state],
        out_specs=[out, out, state],
        out_shape=[jax.ShapeDtypeStruct((N_DIR, T, R_N, G * LANES), F32),
                   jax.ShapeDtypeStruct((N_DIR, T, R_N, G * LANES), F32),
                   jax.ShapeDtypeStruct((N_DIR, G, R_N, R_N, LANES), F32)],
        scratch_shapes=[scratch] * 6,
        compiler_params=_params(3),
        name="rwkv_scan",
    )(xs, xs, xs, xs, xs, p_dir, p_sh, s0)


def _post_kernel(y_ref, c_ref, z_ref, ln_ref, o_ref, *, n_b):
    n_tok = y_ref.shape[1]
    per_xpose = 2 if n_b == SUBLANES else 1
    lo = lax.broadcasted_iota(jnp.int32, (R_N, LANES), 1) < R_N

    def normed(t, g):
        lanes = pl.ds(g * LANES, LANES)
        y = y_ref[0, t, :, lanes] + y_ref[1, t, :, lanes]
        mean = jnp.mean(y, axis=0, keepdims=True)
        yc = y - mean
        var = jnp.mean(yc * yc, axis=0, keepdims=True)
        return (yc * lax.rsqrt(var + LNX_EPS) * ln_ref[0, :, lanes] + ln_ref[1, :, lanes]
                + c_ref[0, t, :, lanes] + c_ref[1, t, :, lanes])

    def body(tp, carry):
        t0 = tp * per_xpose
        if n_b == SUBLANES:
            a, b = normed(t0, 0), normed(t0 + 1, 0)
            xt = jnp.concatenate([jnp.where(lo, a, pltpu.roll(b, R_N, 1)),
                                  jnp.where(lo, pltpu.roll(a, R_N, 1), b)], axis=0)
            x = xt.T
            for i in range(2):
                rows = x[i * R_N:(i + 1) * R_N].reshape(HEAD_PAIRS, n_b, LANES)
                o_ref[t0 + i] = rows * z_ref[t0 + i]
        else:
            x = jnp.concatenate([normed(t0, 0), normed(t0, 1)], axis=0).T
            o_ref[t0] = x.reshape(HEAD_PAIRS, n_b, LANES) * z_ref[t0]
        return carry

    lax.fori_loop(0, n_tok // per_xpose, body, 0)


def _rwkv_post(y, c, xs, ln):
    _, T, _, CH = y.shape
    B = xs.shape[3]
    tb = min(RWKV_TB, T)
    chain = pl.BlockSpec((N_DIR, tb, R_N, CH), lambda t: (0, t, 0, 0))
    nat = pl.BlockSpec((None, tb, HEAD_PAIRS, B, LANES), lambda t: (N_SLOTS - 1, t, 0, 0, 0))
    return pl.pallas_call(
        functools.partial(_post_kernel, n_b=B),
        grid=(T // tb,),
        in_specs=[chain, chain, nat, pl.BlockSpec((2, R_N, CH), lambda t: (0, 0, 0))],
        out_specs=pl.BlockSpec((tb, HEAD_PAIRS, B, LANES), lambda t: (t, 0, 0, 0)),
        out_shape=jax.ShapeDtypeStruct((T, HEAD_PAIRS, B, LANES), F32),
        compiler_params=_params(1),
        name="rwkv_post",
    )(y, c, xs, ln)


OUT_TM = 512


def _out_kernel(hm_ref, yr_ref, w_ref, x_ref, gate_ref, g_ref, o_ref):
    cat = jnp.concatenate([hm_ref[...], yr_ref[...].astype(BF16)], axis=1)
    out = jnp.dot(cat, w_ref[...], preferred_element_type=F32)
    z = x_ref[...] + gate_ref[...] * out
    o_ref[...] = z * lax.rsqrt(jnp.mean(z * z, axis=-1, keepdims=True) + EPS) * g_ref[...]


def _out_proj(hm, yr, w_out, x, gate, final_g):
    B, T, D = x.shape
    tm = min(OUT_TM, T)
    per_b = gate.shape[0] == B
    row = lambda b, t: (b, t, 0)
    return pl.pallas_call(
        _out_kernel,
        grid=(B, T // tm),
        in_specs=[pl.BlockSpec((None, tm, M_WIDTH), row),
                  pl.BlockSpec((None, tm, R_WIDTH), row),
                  pl.BlockSpec((D, D), lambda b, t: (0, 0)),
                  pl.BlockSpec((None, tm, D), row),
                  pl.BlockSpec((None, 1, D), lambda b, t: (b if per_b else 0, 0, 0)),
                  pl.BlockSpec((1, D), lambda b, t: (0, 0))],
        out_specs=pl.BlockSpec((None, tm, D), row),
        out_shape=jax.ShapeDtypeStruct((B, T, D), F32),
        compiler_params=_params(2),
        name="out_proj",
    )(hm, yr, w_out, x, gate, final_g)


def _rmsnorm(x, g):
    return x * lax.rsqrt(jnp.mean(x * x, axis=-1, keepdims=True) + EPS) * g


def _chain_param(p, n_b):
    q = p.reshape(HEAD_PAIRS, 2, R_N)
    if n_b == SUBLANES:
        t = jnp.broadcast_to(q.transpose(2, 1, 0)[..., None], (R_N, 2, HEAD_PAIRS, n_b))
        return t.reshape(1, R_N, LANES)
    t = jnp.broadcast_to(q.transpose(1, 2, 0)[..., None], (2, R_N, HEAD_PAIRS, n_b))
    return t.reshape(2, R_N, LANES)


def _chain_state(s, n_b):
    s6 = s.reshape(n_b, N_DIR, HEAD_PAIRS, 2, R_N, R_N)
    if n_b == SUBLANES:
        return s6.transpose(1, 5, 4, 3, 2, 0).reshape(N_DIR, 1, R_N, R_N, LANES)
    return s6.transpose(1, 3, 5, 4, 2, 0).reshape(N_DIR, 2, R_N, R_N, LANES)


def _unchain_state(s, n_b):
    if n_b == SUBLANES:
        s6 = s.reshape(N_DIR, R_N, R_N, 2, HEAD_PAIRS, n_b).transpose(5, 0, 4, 3, 2, 1)
    else:
        s6 = s.reshape(N_DIR, 2, R_N, R_N, HEAD_PAIRS, n_b).transpose(5, 0, 4, 1, 3, 2)
    return s6.reshape(n_b, N_DIR, R_HEADS, R_N, R_N)


def _block(x, mod, st, lw, norm_g, final_g, grid):
    (w_mproj, w_rproj, m_conv_w, m_conv_b, m_gate_b, m_ln_g, r_mu, r_w0, w_lora, r_a0,
     r_k_k, r_k_a, r_r_k, r_ln_g, r_ln_b, w_out) = lw
    C0, n0, m0, S0 = st
    B, T, D = x.shape
    assert B in (SUBLANES, 2 * SUBLANES), "chain layout is written for 8 or 16 sequences"
    shift, scale, gate = jnp.split(mod, 3, axis=-1)
    h = (_rmsnorm(x, norm_g) * (1.0 + scale) + shift).astype(BF16)
    tm = min(512, B * T)
    p = _mm(h.reshape(B * T, D), w_mproj, tm, M_PROJ_COLS // 2).reshape(B, T, M_PROJ_COLS)
    pr = _mm(h.transpose(1, 0, 2).reshape(T * B, D), w_rproj, tm, R_PROJ_COLS // 2)
    pr = pr.reshape(T, B, R_PROJ_COLS)
    mg = p[..., 5 * M_WIDTH:5 * M_WIDTH + M_GATE_COLS]

    gates = (mg.reshape(B, T, N_DIR, 2, M_HEADS) + m_gate_b).transpose(2, 3, 0, 4, 1)
    nc = T // CHUNK
    log_i = gates[:, 0].reshape(N_DIR, B, M_HEADS, nc, CHUNK)
    log_f = jax.nn.log_sigmoid(gates[:, 1]).reshape(N_DIR, B, M_HEADS, nc, CHUNK)
    hm, Cn, nn, mn = _mlstm(
        p, m_conv_w, m_conv_b.reshape(1, 2 * M_WIDTH), m_ln_g.reshape(1, M_WIDTH), log_i, log_f,
        C0, n0[..., None, :], jnp.broadcast_to(m0[..., None, None], m0.shape + (1, LANES)))

    xs = _rwkv_prep(pr, r_mu.reshape(1, SHIFT_COLS), w_lora, grid)
    p_dir = jnp.stack([jnp.stack([_chain_param(r_w0[z], B), _chain_param(r_a0[z], B)], axis=1)
                       for z in range(N_DIR)])
    p_sh = jnp.stack([_chain_param(r_k_k, B), _chain_param(r_k_a, B),
                      _chain_param(r_r_k.reshape(R_WIDTH), B)], axis=1)
    y_dirs, c_dirs, S_new = _rwkv(xs, p_dir, p_sh, _chain_state(S0, B))
    G = p_sh.shape[0]
    ln = jnp.stack([_chain_param(r_ln_g, B), _chain_param(r_ln_b, B)])
    ln = ln.transpose(0, 2, 1, 3).reshape(2, R_N, G * LANES)
    yr = _rwkv_post(y_dirs, c_dirs, xs, ln)
    yr = yr.transpose(2, 0, 1, 3).reshape(B, T, R_WIDTH)

    out = _out_proj(hm, yr, w_out, x, gate, final_g.reshape(1, D))
    return out, (Cn, nn[..., 0, :], mn[..., 0, 0], _unchain_state(S_new, B))


def kernel(x_prompt, x_sample, state_mlstm_C, state_mlstm_n, state_mlstm_m, state_rwkv_S, c, c_ctx,
           norm_g, w_ada, b_ada, w_in, m_conv_w, m_conv_b, m_gate_b, m_ln_g, r_mu, r_w0, r_w2, r_a0,
           r_a2, r_k_k, r_k_a, r_r_k, r_ln_g, r_ln_b, w_out, final_g):
    depth = w_in.shape[0]
    assert depth == 1, "the final RMSNorm is fused into the single layer's output projection"
    bp = x_prompt.shape[0]
    ctx_state0 = (jnp.zeros((bp, N_DIR, M_HEADS, M_DK, M_DK), F32),
                  jnp.zeros((bp, N_DIR, M_HEADS, M_DK), F32),
                  jnp.full((bp, N_DIR, M_HEADS), -jnp.inf, F32),
                  jnp.zeros((bp, N_DIR, R_HEADS, R_N, R_N), F32))
    l = 0
    w_l = w_in[l]
    gate_lo = 5 * M_WIDTH
    gate_hi = gate_lo + M_GATE_COLS
    w_mproj = jnp.concatenate(
        [w_l[:, :gate_hi], jnp.zeros((D_MODEL, GATE_PAD - M_GATE_COLS), F32)], axis=1).astype(BF16)
    w_rproj = w_l[:, gate_hi:].astype(BF16)
    zpad = jnp.zeros((LORA, R_WIDTH), F32)
    w_lora = jnp.stack([jnp.concatenate([r_w2[l, 0], zpad]), jnp.concatenate([zpad, r_w2[l, 1]]),
                        jnp.concatenate([r_a2[l, 0], zpad]), jnp.concatenate([zpad, r_a2[l, 1]])])
    lw = (w_mproj, w_rproj, m_conv_w[l], m_conv_b[l], m_gate_b[l], m_ln_g[l], r_mu[l], r_w0[l], w_lora,
          r_a0[l], r_k_k[l], r_k_a[l], r_r_k[l], r_ln_g[l], r_ln_b[l], w_out[l].astype(BF16))
    n_cond = 1 + c.shape[0]
    cond = jnp.concatenate([c_ctx[None, :], c, jnp.zeros((16 - n_cond, D_MODEL), F32)], axis=0)
    mod = _mm(jax.nn.silu(cond), w_ada[l], 16, 1536)[:n_cond] + b_ada[l]
    mod_p = mod[0:1, None, :]
    mod_s = mod[1:, None, :]
    y_prompt, (Cp, np_, mp, Sp) = _block(x_prompt, mod_p, ctx_state0, lw, norm_g[l], final_g, False)
    st_s = (state_mlstm_C[:, l], state_mlstm_n[:, l], state_mlstm_m[:, l], state_rwkv_S[:, l])
    y_sample, _ = _block(x_sample, mod_s, st_s, lw, norm_g[l], final_g, True)
    return (y_prompt, y_sample, Cp[:, None], np_[:, None], mp[:, None], Sp[:, None])
```

```python
import functools

import jax
import jax.numpy as jnp
import numpy as np
from jax import lax
from jax.experimental import pallas as pl
from jax.experimental.pallas import tpu as pltpu

D_MODEL = 2048
GRID_W = 64
N_DIR = 2
M_WIDTH = D_MODEL // 2
M_HEADS = 4
M_DK = M_WIDTH // M_HEADS
R_WIDTH = D_MODEL - M_WIDTH
R_N = 64
R_HEADS = R_WIDTH // R_N
LORA = 64
CONV_K = 3
CHUNK = 64
EPS = 1e-6
LNX_EPS = 64e-5
M_GATE_COLS = N_DIR * 2 * M_HEADS
SHIFT_COLS = 3 * R_WIDTH + 2 * N_DIR * LORA

LANES = 128
SUBLANES = 8
HEAD_PAIRS = R_WIDTH // LANES
GATE_PAD = 2 * LANES
M_PROJ_COLS = 5 * M_WIDTH + GATE_PAD
R_PROJ_COLS = R_WIDTH + SHIFT_COLS
RZ_BLK0 = 0
RS_BLK0 = R_WIDTH // LANES
VMEM_LIMIT = 48 * 1024 * 1024

F32 = jnp.float32
BF16 = jnp.bfloat16


def _params(n_axes):
    return pltpu.CompilerParams(dimension_semantics=("arbitrary",) * n_axes,
                                vmem_limit_bytes=VMEM_LIMIT)


def _sigmoid(x):
    return 1.0 / (1.0 + jnp.exp(-x))


def _silu(x):
    return x * _sigmoid(x)


def _softplus(x):
    return jnp.maximum(x, 0.0) + jnp.log(1.0 + jnp.exp(-jnp.abs(x)))


def _mm_kernel(x_ref, w_ref, o_ref):
    o_ref[...] = jnp.dot(x_ref[...].astype(BF16), w_ref[...].astype(BF16),
                         preferred_element_type=F32)


def _mm(x, w, tm, tn):
    M, K = x.shape
    N = w.shape[1]
    assert M % tm == 0 and N % tn == 0
    return pl.pallas_call(
        _mm_kernel,
        grid=(N // tn, M // tm),
        in_specs=[pl.BlockSpec((tm, K), lambda j, i: (i, 0)),
                  pl.BlockSpec((K, tn), lambda j, i: (0, j))],
        out_specs=pl.BlockSpec((tm, tn), lambda j, i: (i, j)),
        out_shape=jax.ShapeDtypeStruct((M, N), F32),
        compiler_params=_params(2),
        name="proj_mm",
    )(x, w)


MLSTM_UNROLL = 2


def _mlstm_kernel(q_ref, k_ref, v_ref, o_ref, z_ref, wq_ref, wk_ref, bq_ref, bk_ref, lng_ref,
                  gi_ref, gf_ref, c0_ref, n0_ref, m0_ref,
                  hm_ref, c_ref, n_ref, m_ref, qs, ks, hs_f, hs_b):
    n_tok = q_ref.shape[0]
    nc = n_tok // CHUNK
    L = CHUNK
    tpos = lax.broadcasted_iota(jnp.int32, (n_tok, M_DK), 0)

    def conv_silu(x_ref, w_ref, b_ref):
        x = x_ref[...]
        prev = jnp.where(tpos == 0, 0.0, pltpu.roll(x, 1, 0))
        nxt = jnp.where(tpos == n_tok - 1, 0.0, pltpu.roll(x, n_tok - 1, 0))
        return _silu(prev * w_ref[0:1, :] + x * w_ref[1:2, :] + nxt * w_ref[2:3, :] + b_ref[...])

    qs[...] = conv_silu(q_ref, wq_ref, bq_ref)
    ks[...] = conv_silu(k_ref, wk_ref, bk_ref) * (M_DK ** -0.5)
    c_ref[...] = c0_ref[...]

    row = lax.broadcasted_iota(jnp.int32, (L, L), 0)
    col = lax.broadcasted_iota(jnp.int32, (L, L), 1)
    eye = row == col
    neg_inf = jnp.float32(-jnp.inf)

    def chunk(dirn, cc, n, m):
        seen = col <= row if dirn == 0 else col >= row
        seen_t = row <= col if dirn == 0 else row >= col
        r0 = pl.multiple_of(cc * L, L)
        q = qs[pl.ds(r0, L), :]
        k = ks[pl.ds(r0, L), :]
        v = v_ref[pl.ds(r0, L), :]
        f_row = gf_ref[dirn, pl.ds(cc, 1), :]
        i_row = gi_ref[dirn, pl.ds(cc, 1), :]
        f_col = jnp.sum(jnp.where(eye, f_row, 0.0), axis=1, keepdims=True)
        i_col = jnp.sum(jnp.where(eye, i_row, 0.0), axis=1, keepdims=True)
        b_col = jnp.sum(jnp.where(seen, f_row, 0.0), axis=1, keepdims=True)
        b_row = jnp.sum(jnp.where(seen_t, f_col, 0.0), axis=0, keepdims=True)
        dmat = jnp.where(seen, b_col - b_row + i_row, neg_inf)
        inter = b_col + m
        mt = jnp.maximum(inter, jnp.max(dmat, axis=1, keepdims=True))
        qb = q.astype(BF16)
        kb = k.astype(BF16)
        vb = v.astype(BF16)
        qk = lax.dot_general(qb, kb, (((1,), (1,)), ((), ())), preferred_element_type=F32)
        a = jnp.exp(dmat - mt) * qk
        s_in = jnp.exp(inter - mt)
        c = c_ref[dirn]
        qc = jnp.dot(qb, c.astype(BF16), preferred_element_type=F32)
        av = jnp.dot(a.astype(BF16), vb, preferred_element_type=F32)
        num = s_in * qc + av
        den = s_in * jnp.sum(q * n, axis=1, keepdims=True) + jnp.sum(a, axis=1, keepdims=True)
        hs = hs_f if dirn == 0 else hs_b
        hs[pl.ds(r0, L), :] = num / jnp.maximum(jnp.abs(den), jnp.exp(-mt))
        b_last = jnp.sum(f_row, axis=1, keepdims=True)
        g_row = b_last - b_row + i_row
        g_col = b_last - b_col + i_col
        m_new = jnp.maximum(b_last + m, jnp.max(g_row, axis=1, keepdims=True))
        decay = jnp.exp(b_last + m - m_new)
        wk = jnp.exp(g_col - m_new) * k
        c_ref[dirn] = decay * c + lax.dot_general(
            wk.astype(BF16), vb, (((0,), (0,)), ((), ())), preferred_element_type=F32)
        n_new = decay * n + jnp.sum(wk, axis=0, keepdims=True)
        return n_new, m_new

    def body(ci, carry):
        n_f, m_f, n_b, m_b = carry
        n_f, m_f = chunk(0, ci, n_f, m_f)
        n_b, m_b = chunk(1, nc - 1 - ci, n_b, m_b)
        return n_f, m_f, n_b, m_b

    n_f, m_f, n_b, m_b = lax.fori_loop(
        0, nc, body, (n0_ref[0], m0_ref[0, :, 0:1], n0_ref[1], m0_ref[1, :, 0:1]),
        unroll=MLSTM_UNROLL)
    n_ref[0] = n_f
    n_ref[1] = n_b
    m_ref[0] = jnp.broadcast_to(m_f, (1, LANES))
    m_ref[1] = jnp.broadcast_to(m_b, (1, LANES))

    hm = _sigmoid(o_ref[...]) * (hs_f[...] + hs_b[...])
    hc = hm - jnp.mean(hm, axis=1, keepdims=True)
    hn = hc * lax.rsqrt(jnp.mean(hc * hc, axis=1, keepdims=True) + EPS)
    hm_ref[...] = (hn * lng_ref[...] * _silu(z_ref[...])).astype(BF16)


def _mlstm(p, conv_w, conv_b, ln_g, gi, gf, c0, n0, m0):
    B, T, _ = p.shape
    nc = T // CHUNK
    tiles_per_part = M_WIDTH // M_DK

    def col_spec(part):
        return pl.BlockSpec((None, T, M_DK), lambda b, h: (b, 0, part * tiles_per_part + h))

    def par_spec(rows, part):
        return pl.BlockSpec((rows, M_DK), lambda b, h: (0, part * tiles_per_part + h))

    gate_spec = pl.BlockSpec((N_DIR, None, None, nc, CHUNK), lambda b, h: (0, b, h, 0, 0))
    c_spec = pl.BlockSpec((None, N_DIR, None, M_DK, M_DK), lambda b, h: (b, 0, h, 0, 0))
    n_spec = pl.BlockSpec((None, N_DIR, None, 1, M_DK), lambda b, h: (b, 0, h, 0, 0))
    m_spec = pl.BlockSpec((None, N_DIR, None, 1, LANES), lambda b, h: (b, 0, h, 0, 0))
    seq = pltpu.VMEM((T, M_DK), F32)
    return pl.pallas_call(
        _mlstm_kernel,
        grid=(B, M_HEADS),
        in_specs=[col_spec(0), col_spec(1), col_spec(2), col_spec(3), col_spec(4),
                  par_spec(CONV_K, 0), par_spec(CONV_K, 1), par_spec(1, 0), par_spec(1, 1),
                  par_spec(1, 0), gate_spec, gate_spec, c_spec, n_spec, m_spec],
        out_specs=[pl.BlockSpec((None, T, M_DK), lambda b, h: (b, 0, h)), c_spec, n_spec, m_spec],
        out_shape=[jax.ShapeDtypeStruct((B, T, M_WIDTH), BF16),
                   jax.ShapeDtypeStruct((B, N_DIR, M_HEADS, M_DK, M_DK), F32),
                   jax.ShapeDtypeStruct((B, N_DIR, M_HEADS, 1, M_DK), F32),
                   jax.ShapeDtypeStruct((B, N_DIR, M_HEADS, 1, LANES), F32)],
        scratch_shapes=[seq, seq, seq, seq],
        compiler_params=_params(2),
        name="mlstm_scan",
    )(p, p, p, p, p, conv_w, conv_w, conv_b, conv_b, ln_g, gi, gf, c0, n0, m0)


N_SLOTS = 8
MIX_STEPS = 3 * HEAD_PAIRS
LORA_STEPS = 2 * N_DIR * HEAD_PAIRS
PREP_STEPS = MIX_STEPS + LORA_STEPS + HEAD_PAIRS


PREP_TT = 512


def _prep_kernel(x_ref, lo_ref, hi_ref, mu_ref, w_ref, o_ref, *, grid_shift, n_tok):
    tblk = pl.program_id(0)
    c = pl.program_id(1)
    x = x_ref[...]
    tt = x.shape[0]
    t = tblk * tt + lax.broadcasted_iota(jnp.int32, x.shape, 0)
    lane4 = lax.broadcasted_iota(jnp.int32, x.shape, 2) % 4

    def mixed():
        xe = jnp.concatenate([lo_ref[...], x, hi_ref[...]], axis=0)
        prev = xe[GRID_W - 1:GRID_W - 1 + tt]
        nxt = xe[GRID_W + 1:GRID_W + 1 + tt]
        if grid_shift:
            col = t % GRID_W
            left = jnp.where(col == 0, 0.0, prev)
            right = jnp.where(col == GRID_W - 1, 0.0, nxt)
            up = jnp.where(t < GRID_W, 0.0, xe[:tt])
            down = jnp.where(t >= n_tok - GRID_W, 0.0, xe[2 * GRID_W:])
            sh = jnp.where(lane4 == 0, left, jnp.where(lane4 == 1, right, jnp.where(lane4 == 2, up, down)))
        else:
            sh = jnp.where(lane4 % 2 == 0, jnp.where(t == 0, 0.0, prev), jnp.where(t == n_tok - 1, 0.0, nxt))
        return x + mu_ref[...] * (sh - x)

    def project(a):
        rows = a.reshape(tt * a.shape[1], LANES).astype(BF16)
        return jnp.dot(rows, w_ref[...].astype(BF16), preferred_element_type=F32).reshape(a.shape)

    n_b = x.shape[1]

    def emit_scan_input(val):
        v4 = val.reshape(tt // 2, 2, n_b, LANES)
        if n_b == SUBLANES:
            even, odd = v4[:, 0], v4[:, 1]
            lo = lax.broadcasted_iota(jnp.int32, even.shape, 2) < R_N
            o_ref[:, 0] = jnp.where(lo, even, pltpu.roll(odd, R_N, 2))
            o_ref[:, 1] = jnp.where(lo, pltpu.roll(even, R_N, 2), odd)
        else:
            o_ref[...] = v4

    @pl.when(c < MIX_STEPS)
    def _():
        emit_scan_input(mixed())

    @pl.when((c >= MIX_STEPS) & (c < MIX_STEPS + N_DIR * HEAD_PAIRS))
    def _():
        emit_scan_input(project(jnp.tanh(mixed())))

    @pl.when((c >= MIX_STEPS + N_DIR * HEAD_PAIRS) & (c < MIX_STEPS + LORA_STEPS))
    def _():
        emit_scan_input(project(mixed()))

    @pl.when(c >= MIX_STEPS + LORA_STEPS)
    def _():
        o_ref[...] = _silu(x).reshape(tt // 2, 2, n_b, LANES)


def _rwkv_prep(p, r_mu, w_lora, grid_shift):
    T, B, _ = p.shape
    tt = min(PREP_TT, T)
    halo_per_blk = tt // GRID_W
    n_halo = T // GRID_W
    lora_lo = MIX_STEPS
    rate_lo = MIX_STEPS + N_DIR * HEAD_PAIRS
    z_lo = MIX_STEPS + LORA_STEPS

    def src_tile(c):
        return jnp.where(c < lora_lo, c, jnp.where(c < rate_lo, MIX_STEPS, MIX_STEPS + 1))

    def col(c):
        return jnp.where(c < z_lo, RS_BLK0 + src_tile(c), RZ_BLK0 + c % HEAD_PAIRS)

    def w_map(t, c):
        return (jnp.clip((c - lora_lo) // HEAD_PAIRS, 0, 2 * N_DIR - 1), 0, c % HEAD_PAIRS)

    def o_map(t, c):
        slot = jnp.where(c < lora_lo, c // HEAD_PAIRS,
                         jnp.where(c < z_lo, 3 + (c - lora_lo) // HEAD_PAIRS, N_SLOTS - 1))
        return (slot, t, 0, c % HEAD_PAIRS, 0, 0)

    return pl.pallas_call(
        functools.partial(_prep_kernel, grid_shift=grid_shift, n_tok=T),
        grid=(T // tt, PREP_STEPS),
        in_specs=[pl.BlockSpec((tt, B, LANES), lambda t, c: (t, 0, col(c))),
                  pl.BlockSpec((GRID_W, B, LANES),
                               lambda t, c: (jnp.maximum(t * halo_per_blk - 1, 0), 0, col(c))),
                  pl.BlockSpec((GRID_W, B, LANES),
                               lambda t, c: (jnp.minimum((t + 1) * halo_per_blk, n_halo - 1), 0, col(c))),
                  pl.BlockSpec((1, LANES), lambda t, c: (0, src_tile(c))),
                  pl.BlockSpec((None, LANES, LANES), w_map)],
        out_specs=pl.BlockSpec((None, tt // 2, 2, None, B, LANES), o_map),
        out_shape=jax.ShapeDtypeStruct((N_SLOTS, T // 2, 2, HEAD_PAIRS, B, LANES), F32),
        compiler_params=_params(2),
        name="rwkv_prep",
    )(p, p, p, r_mu, w_lora)


RWKV_TB = 16
J_UNROLL = 16
PREP_UNROLL = 4


def _chain_tiles(ref, pair, group, n_b):
    if n_b == SUBLANES:
        xt = ref[pair].reshape(LANES, LANES).T
        return [xt[:R_N], xt[R_N:]]
    tiles = []
    for parity in range(2):
        xt = ref[pair, parity].reshape(LANES, LANES).T
        tiles.append(jnp.where(group == 0, xt[:R_N], xt[R_N:]))
    return tiles


def _rwkv_kernel(r_ref, k_ref, v_ref, lw_ref, la_ref, pd_ref, ps_ref, s0_ref,
                 y_ref, c_ref, s_ref, kh_s, rh_s, vv_s, kt_s, bt_s, g_s, *, n_b):
    dirn = pl.program_id(0)
    group = pl.program_id(1)
    tb = pl.program_id(2)
    n_tok = y_ref.shape[0]
    n_pairs = n_tok // 2
    fwd = dirn == 0

    @pl.when(tb == 0)
    def _():
        s_ref[...] = s0_ref[...]

    w0, a0 = pd_ref[0], pd_ref[1]
    k_k, k_a, r_k = ps_ref[0], ps_ref[1], ps_ref[2]

    def prep(q, g):
        pair = q + dirn * (n_pairs - 1 - 2 * q)
        tiles = [_chain_tiles(ref, pair, group, n_b) for ref in (r_ref, k_ref, v_ref, lw_ref, la_ref)]
        for i in range(2):
            r, kraw, v, lw, la = (jnp.where(fwd, x[i], x[1 - i]) for x in tiles)
            pos = 2 * q + i
            tok = 2 * pair + i + dirn * (1 - 2 * i)
            wlog = -_softplus(-(w0 + lw)) - 0.5
            a = _sigmoid(a0 + la)
            kk = kraw * k_k
            kk = kk / jnp.maximum(jnp.sqrt(jnp.sum(kk * kk, axis=0, keepdims=True)), 1e-12)
            kz = kraw * (1.0 + (a - 1.0) * k_a)
            c_ref[tok] = jnp.sum(r * kz * r_k, axis=0, keepdims=True) * v
            kh_s[pos] = kk * g
            g = g * jnp.exp(-jnp.exp(wlog))
            g_inv = 1.0 / g
            vv_s[pos] = v
            bt_s[pos] = kk * a * g_inv
            kt_s[pos] = kz * g_inv
            rh_s[pos] = r * g
        return g

    g_s[...] = lax.fori_loop(0, n_pairs, prep, jnp.ones((R_N, LANES), F32), unroll=PREP_UNROLL)

    zeros = jnp.zeros((R_N, LANES), F32)

    def dot_first(jb, u):
        for jj in range(J_UNROLL):
            j = jb * J_UNROLL + jj
            u = u + s_ref[j] * kh_s[0, pl.ds(j, 1), :]
        return u

    def step(pos, u):
        tok = pos + dirn * (n_tok - 1 - 2 * pos)
        nxt = jnp.minimum(pos + 1, n_tok - 1)
        sa = -u
        vt = vv_s[pos]

        def update(jb, carry):
            y, u_next = carry
            for jj in range(J_UNROLL):
                j = jb * J_UNROLL + jj
                sn = s_ref[j] + sa * bt_s[pos, pl.ds(j, 1), :] + vt * kt_s[pos, pl.ds(j, 1), :]
                s_ref[j] = sn
                y = y + sn * rh_s[pos, pl.ds(j, 1), :]
                u_next = u_next + sn * kh_s[nxt, pl.ds(j, 1), :]
            return y, u_next

        y, u_next = lax.fori_loop(0, R_N // J_UNROLL, update, (zeros, zeros))
        y_ref[tok] = y
        return u_next

    lax.fori_loop(0, n_tok, step, lax.fori_loop(0, R_N // J_UNROLL, dot_first, zeros))

    def rescale(jb, carry):
        for jj in range(J_UNROLL):
            j = jb * J_UNROLL + jj
            s_ref[j] = s_ref[j] * g_s[pl.ds(j, 1), :]
        return carry

    lax.fori_loop(0, R_N // J_UNROLL, rescale, 0)


def _rwkv(xs, p_dir, p_sh, s0):
    _, half_t, _, _, B, _ = xs.shape
    T = 2 * half_t
    G = p_sh.shape[0]
    tb = min(RWKV_TB, T)
    nt = T // tb
    rev = lambda dd, t: t + dd * (nt - 1 - 2 * t)

    def slot(s):
        return pl.BlockSpec((None, tb // 2, 2, HEAD_PAIRS, B, LANES),
                            lambda dd, g, t: (s(dd), rev(dd, t), 0, 0, 0, 0))

    state = pl.BlockSpec((None, None, R_N, R_N, LANES), lambda dd, g, t: (dd, g, 0, 0, 0))
    out = pl.BlockSpec((None, tb, R_N, LANES), lambda dd, g, t: (dd, rev(dd, t), 0, g))
    scratch = pltpu.VMEM((tb, R_N, LANES), F32)
    return pl.pallas_call(
        functools.partial(_rwkv_kernel, n_b=B),
        grid=(N_DIR, G, nt),
        in_specs=[slot(lambda dd: 0), slot(lambda dd: 1), slot(lambda dd: 2),
                  slot(lambda dd: 3 + dd), slot(lambda dd: 5 + dd),
                  pl.BlockSpec((None, None, 2, R_N, LANES), lambda dd, g, t: (dd, g, 0, 0, 0)),
                  pl.BlockSpec((None, 3, R_N, LANES), lambda dd, g, t: (g, 0, 0, 0)),
                  state],
        out_specs=[out, out, state],
        out_shape=[jax.ShapeDtypeStruct((N_DIR, T, R_N, G * LANES), F32),
                   jax.ShapeDtypeStruct((N_DIR, T, R_N, G * LANES), F32),
                   jax.ShapeDtypeStruct((N_DIR, G, R_N, R_N, LANES), F32)],
        scratch_shapes=[scratch] * 5 + [pltpu.VMEM((R_N, LANES), F32)],
        compiler_params=_params(3),
        name="rwkv_scan",
    )(xs, xs, xs, xs, xs, p_dir, p_sh, s0)


def _post_kernel(y_ref, c_ref, z_ref, ln_ref, o_ref, *, n_b):
    n_tok = y_ref.shape[1]
    lo = lax.broadcasted_iota(jnp.int32, (R_N, LANES), 1) < R_N

    def normed(t, g):
        lanes = pl.ds(g * LANES, LANES)
        y = y_ref[0, t, :, lanes] + y_ref[1, t, :, lanes]
        mean = jnp.mean(y, axis=0, keepdims=True)
        yc = y - mean
        var = jnp.mean(yc * yc, axis=0, keepdims=True)
        return (yc * lax.rsqrt(var + LNX_EPS) * ln_ref[0, :, lanes] + ln_ref[1, :, lanes]
                + c_ref[0, t, :, lanes] + c_ref[1, t, :, lanes])

    def body(tp, carry):
        t0 = tp * 2
        if n_b == SUBLANES:
            a, b = normed(t0, 0), normed(t0 + 1, 0)
            xt = jnp.concatenate([jnp.where(lo, a, pltpu.roll(b, R_N, 1)),
                                  jnp.where(lo, pltpu.roll(a, R_N, 1), b)], axis=0)
            x = xt.T
            for i in range(2):
                rows = x[i * R_N:(i + 1) * R_N].reshape(HEAD_PAIRS, n_b, LANES)
                o_ref[t0 + i] = rows * z_ref[tp, i]
        else:
            for i in range(2):
                x = jnp.concatenate([normed(t0 + i, 0), normed(t0 + i, 1)], axis=0).T
                o_ref[t0 + i] = x.reshape(HEAD_PAIRS, n_b, LANES) * z_ref[tp, i]
        return carry

    lax.fori_loop(0, n_tok // 2, body, 0, unroll=2)


def _rwkv_post(y, c, xs, ln):
    _, T, _, CH = y.shape
    B = xs.shape[4]
    tb = min(RWKV_TB, T)
    chain = pl.BlockSpec((N_DIR, tb, R_N, CH), lambda t: (0, t, 0, 0))
    nat = pl.BlockSpec((None, tb // 2, 2, HEAD_PAIRS, B, LANES),
                       lambda t: (N_SLOTS - 1, t, 0, 0, 0, 0))
    return pl.pallas_call(
        functools.partial(_post_kernel, n_b=B),
        grid=(T // tb,),
        in_specs=[chain, chain, nat, pl.BlockSpec((2, R_N, CH), lambda t: (0, 0, 0))],
        out_specs=pl.BlockSpec((tb, HEAD_PAIRS, B, LANES), lambda t: (t, 0, 0, 0)),
        out_shape=jax.ShapeDtypeStruct((T, HEAD_PAIRS, B, LANES), F32),
        compiler_params=_params(1),
        name="rwkv_post",
    )(y, c, xs, ln)


OUT_TM = 512


def _out_kernel(hm_ref, yr_ref, w_ref, x_ref, gate_ref, g_ref, o_ref):
    cat = jnp.concatenate([hm_ref[...], yr_ref[...].astype(BF16)], axis=1)
    out = jnp.dot(cat, w_ref[...], preferred_element_type=F32)
    z = x_ref[...] + gate_ref[...] * out
    o_ref[...] = z * lax.rsqrt(jnp.mean(z * z, axis=-1, keepdims=True) + EPS) * g_ref[...]


def _out_proj(hm, yr, w_out, x, gate, final_g):
    B, T, D = x.shape
    tm = min(OUT_TM, T)
    per_b = gate.shape[0] == B
    row = lambda b, t: (b, t, 0)
    return pl.pallas_call(
        _out_kernel,
        grid=(B, T // tm),
        in_specs=[pl.BlockSpec((None, tm, M_WIDTH), row),
                  pl.BlockSpec((None, tm, R_WIDTH), row),
                  pl.BlockSpec((D, D), lambda b, t: (0, 0)),
                  pl.BlockSpec((None, tm, D), row),
                  pl.BlockSpec((None, 1, D), lambda b, t: (b if per_b else 0, 0, 0)),
                  pl.BlockSpec((1, D), lambda b, t: (0, 0))],
        out_specs=pl.BlockSpec((None, tm, D), row),
        out_shape=jax.ShapeDtypeStruct((B, T, D), F32),
        compiler_params=_params(2),
        name="out_proj",
    )(hm, yr, w_out, x, gate, final_g)


def _rmsnorm(x, g):
    return x * lax.rsqrt(jnp.mean(x * x, axis=-1, keepdims=True) + EPS) * g


def _chain_param(p, n_b):
    q = p.reshape(HEAD_PAIRS, 2, R_N)
    if n_b == SUBLANES:
        t = jnp.broadcast_to(q.transpose(2, 1, 0)[..., None], (R_N, 2, HEAD_PAIRS, n_b))
        return t.reshape(1, R_N, LANES)
    t = jnp.broadcast_to(q.transpose(1, 2, 0)[..., None], (2, R_N, HEAD_PAIRS, n_b))
    return t.reshape(2, R_N, LANES)


def _chain_state(s, n_b):
    s6 = s.reshape(n_b, N_DIR, HEAD_PAIRS, 2, R_N, R_N)
    if n_b == SUBLANES:
        return s6.transpose(1, 5, 4, 3, 2, 0).reshape(N_DIR, 1, R_N, R_N, LANES)
    return s6.transpose(1, 3, 5, 4, 2, 0).reshape(N_DIR, 2, R_N, R_N, LANES)


def _unchain_state(s, n_b):
    if n_b == SUBLANES:
        s6 = s.reshape(N_DIR, R_N, R_N, 2, HEAD_PAIRS, n_b).transpose(5, 0, 4, 3, 2, 1)
    else:
        s6 = s.reshape(N_DIR, 2, R_N, R_N, HEAD_PAIRS, n_b).transpose(5, 0, 4, 1, 3, 2)
    return s6.reshape(n_b, N_DIR, R_HEADS, R_N, R_N)


def _block(x, mod, st, lw, norm_g, final_g, grid):
    (w_mproj, w_rproj, m_conv_w, m_conv_b, m_gate_b, m_ln_g, r_mu, r_w0, w_lora, r_a0,
     r_k_k, r_k_a, r_r_k, r_ln_g, r_ln_b, w_out) = lw
    C0, n0, m0, S0 = st
    B, T, D = x.shape
    assert B in (SUBLANES, 2 * SUBLANES), "chain layout is written for 8 or 16 sequences"
    shift, scale, gate = jnp.split(mod, 3, axis=-1)
    h = (_rmsnorm(x, norm_g) * (1.0 + scale) + shift).astype(BF16)
    tm = min(512, B * T)
    p = _mm(h.reshape(B * T, D), w_mproj, tm, M_PROJ_COLS // 2).reshape(B, T, M_PROJ_COLS)
    pr = _mm(h.transpose(1, 0, 2).reshape(T * B, D), w_rproj, tm, R_PROJ_COLS // 2)
    pr = pr.reshape(T, B, R_PROJ_COLS)
    mg = p[..., 5 * M_WIDTH:5 * M_WIDTH + M_GATE_COLS]

    gates = (mg.reshape(B, T, N_DIR, 2, M_HEADS) + m_gate_b).transpose(2, 3, 0, 4, 1)
    nc = T // CHUNK
    log_i = gates[:, 0].reshape(N_DIR, B, M_HEADS, nc, CHUNK)
    log_f = jax.nn.log_sigmoid(gates[:, 1]).reshape(N_DIR, B, M_HEADS, nc, CHUNK)
    hm, Cn, nn, mn = _mlstm(
        p, m_conv_w, m_conv_b.reshape(1, 2 * M_WIDTH), m_ln_g.reshape(1, M_WIDTH), log_i, log_f,
        C0, n0[..., None, :], jnp.broadcast_to(m0[..., None, None], m0.shape + (1, LANES)))

    xs = _rwkv_prep(pr, r_mu.reshape(1, SHIFT_COLS), w_lora, grid)
    p_dir = jnp.stack([jnp.stack([_chain_param(r_w0[z], B), _chain_param(r_a0[z], B)], axis=1)
                       for z in range(N_DIR)])
    p_sh = jnp.stack([_chain_param(r_k_k, B), _chain_param(r_k_a, B),
                      _chain_param(r_r_k.reshape(R_WIDTH), B)], axis=1)
    y_dirs, c_dirs, S_new = _rwkv(xs, p_dir, p_sh, _chain_state(S0, B))
    G = p_sh.shape[0]
    ln = jnp.stack([_chain_param(r_ln_g, B), _chain_param(r_ln_b, B)])
    ln = ln.transpose(0, 2, 1, 3).reshape(2, R_N, G * LANES)
    yr = _rwkv_post(y_dirs, c_dirs, xs, ln)
    yr = yr.transpose(2, 0, 1, 3).reshape(B, T, R_WIDTH)

    out = _out_proj(hm, yr, w_out, x, gate, final_g.reshape(1, D))
    return out, (Cn, nn[..., 0, :], mn[..., 0, 0], _unchain_state(S_new, B))


def kernel(x_prompt, x_sample, state_mlstm_C, state_mlstm_n, state_mlstm_m, state_rwkv_S, c, c_ctx,
           norm_g, w_ada, b_ada, w_in, m_conv_w, m_conv_b, m_gate_b, m_ln_g, r_mu, r_w0, r_w2, r_a0,
           r_a2, r_k_k, r_k_a, r_r_k, r_ln_g, r_ln_b, w_out, final_g):
    depth = w_in.shape[0]
    assert depth == 1, "the final RMSNorm is fused into the single layer's output projection"
    bp = x_prompt.shape[0]
    ctx_state0 = (jnp.zeros((bp, N_DIR, M_HEADS, M_DK, M_DK), F32),
                  jnp.zeros((bp, N_DIR, M_HEADS, M_DK), F32),
                  jnp.full((bp, N_DIR, M_HEADS), -jnp.inf, F32),
                  jnp.zeros((bp, N_DIR, R_HEADS, R_N, R_N), F32))
    l = 0
    w_l = w_in[l]
    gate_lo = 5 * M_WIDTH
    gate_hi = gate_lo + M_GATE_COLS
    w_mproj = jnp.concatenate(
        [w_l[:, :gate_hi], jnp.zeros((D_MODEL, GATE_PAD - M_GATE_COLS), F32)], axis=1).astype(BF16)
    w_rproj = w_l[:, gate_hi:].astype(BF16)
    zpad = jnp.zeros((LORA, R_WIDTH), F32)
    w_lora = jnp.stack([jnp.concatenate([r_w2[l, 0], zpad]), jnp.concatenate([zpad, r_w2[l, 1]]),
                        jnp.concatenate([r_a2[l, 0], zpad]), jnp.concatenate([zpad, r_a2[l, 1]])])
    lw = (w_mproj, w_rproj, m_conv_w[l], m_conv_b[l], m_gate_b[l], m_ln_g[l], r_mu[l], r_w0[l], w_lora,
          r_a0[l], r_k_k[l], r_k_a[l], r_r_k[l], r_ln_g[l], r_ln_b[l], w_out[l].astype(BF16))
    n_cond = 1 + c.shape[0]
    cond = jnp.concatenate([c_ctx[None, :], c, jnp.zeros((16 - n_cond, D_MODEL), F32)], axis=0)
    mod = _mm(jax.nn.silu(cond), w_ada[l], 16, 1536)[:n_cond] + b_ada[l]
    mod_p = mod[0:1, None, :]
    mod_s = mod[1:, None, :]
    y_prompt, (Cp, np_, mp, Sp) = _block(x_prompt, mod_p, ctx_state0, lw, norm_g[l], final_g, False)
    st_s = (state_mlstm_C[:, l], state_mlstm_n[:, l], state_mlstm_m[:, l], state_rwkv_S[:, l])
    y_sample, _ = _block(x_sample, mod_s, st_s, lw, norm_g[l], final_g, True)
    return (y_prompt, y_sample, Cp[:, None], np_[:, None], mp[:, None], Sp[:, None])
```

```python
import functools

import jax
import jax.numpy as jnp
import numpy as np
from jax import lax
from jax.experimental import pallas as pl
from jax.experimental.pallas import tpu as pltpu

D_MODEL = 2048
GRID_W = 64
N_DIR = 2
M_WIDTH = D_MODEL // 2
M_HEADS = 4
M_DK = M_WIDTH // M_HEADS
R_WIDTH = D_MODEL - M_WIDTH
R_N = 64
R_HEADS = R_WIDTH // R_N
LORA = 64
CONV_K = 3
CHUNK = 64
EPS = 1e-6
LNX_EPS = 64e-5
M_GATE_COLS = N_DIR * 2 * M_HEADS
SHIFT_COLS = 3 * R_WIDTH + 2 * N_DIR * LORA

LANES = 128
SUBLANES = 8
HEAD_PAIRS = R_WIDTH // LANES
GATE_PAD = 2 * LANES
M_PROJ_COLS = 5 * M_WIDTH + GATE_PAD
R_PROJ_COLS = R_WIDTH + SHIFT_COLS
RZ_BLK0 = 0
RS_BLK0 = R_WIDTH // LANES
VMEM_LIMIT = 48 * 1024 * 1024

F32 = jnp.float32
BF16 = jnp.bfloat16


def _params(n_axes):
    return pltpu.CompilerParams(dimension_semantics=("arbitrary",) * n_axes,
                                vmem_limit_bytes=VMEM_LIMIT)


def _sigmoid(x):
    return 1.0 / (1.0 + jnp.exp(-x))


def _silu(x):
    return x * _sigmoid(x)


def _softplus(x):
    return jnp.maximum(x, 0.0) + jnp.log(1.0 + jnp.exp(-jnp.abs(x)))


def _mm_kernel(x_ref, w_ref, o_ref):
    o_ref[...] = jnp.dot(x_ref[...].astype(BF16), w_ref[...].astype(BF16),
                         preferred_element_type=F32)


def _mm(x, w, tm, tn):
    M, K = x.shape
    N = w.shape[1]
    assert M % tm == 0 and N % tn == 0
    return pl.pallas_call(
        _mm_kernel,
        grid=(N // tn, M // tm),
        in_specs=[pl.BlockSpec((tm, K), lambda j, i: (i, 0)),
                  pl.BlockSpec((K, tn), lambda j, i: (0, j))],
        out_specs=pl.BlockSpec((tm, tn), lambda j, i: (i, j)),
        out_shape=jax.ShapeDtypeStruct((M, N), F32),
        compiler_params=_params(2),
        name="proj_mm",
    )(x, w)


MLSTM_UNROLL = 2


def _mlstm_kernel(q_ref, k_ref, v_ref, o_ref, z_ref, wq_ref, wk_ref, bq_ref, bk_ref, lng_ref,
                  gi_ref, gf_ref, c0_ref, n0_ref, m0_ref,
                  hm_ref, c_ref, n_ref, m_ref, qs, ks, hs_f, hs_b):
    n_tok = q_ref.shape[0]
    nc = n_tok // CHUNK
    L = CHUNK
    tpos = lax.broadcasted_iota(jnp.int32, (n_tok, M_DK), 0)

    def conv_silu(x_ref, w_ref, b_ref):
        x = x_ref[...]
        prev = jnp.where(tpos == 0, 0.0, pltpu.roll(x, 1, 0))
        nxt = jnp.where(tpos == n_tok - 1, 0.0, pltpu.roll(x, n_tok - 1, 0))
        return _silu(prev * w_ref[0:1, :] + x * w_ref[1:2, :] + nxt * w_ref[2:3, :] + b_ref[...])

    qs[...] = conv_silu(q_ref, wq_ref, bq_ref)
    ks[...] = conv_silu(k_ref, wk_ref, bk_ref) * (M_DK ** -0.5)
    c_ref[...] = c0_ref[...]

    row = lax.broadcasted_iota(jnp.int32, (L, L), 0)
    col = lax.broadcasted_iota(jnp.int32, (L, L), 1)
    eye = row == col
    neg_inf = jnp.float32(-jnp.inf)

    def chunk(dirn, cc, n, m):
        seen = col <= row if dirn == 0 else col >= row
        seen_t = row <= col if dirn == 0 else row >= col
        r0 = pl.multiple_of(cc * L, L)
        q = qs[pl.ds(r0, L), :]
        k = ks[pl.ds(r0, L), :]
        v = v_ref[pl.ds(r0, L), :]
        f_row = gf_ref[dirn, pl.ds(cc, 1), :]
        i_row = gi_ref[dirn, pl.ds(cc, 1), :]
        f_col = jnp.sum(jnp.where(eye, f_row, 0.0), axis=1, keepdims=True)
        i_col = jnp.sum(jnp.where(eye, i_row, 0.0), axis=1, keepdims=True)
        b_col = jnp.sum(jnp.where(seen, f_row, 0.0), axis=1, keepdims=True)
        b_row = jnp.sum(jnp.where(seen_t, f_col, 0.0), axis=0, keepdims=True)
        dmat = jnp.where(seen, b_col - b_row + i_row, neg_inf)
        inter = b_col + m
        mt = jnp.maximum(inter, jnp.max(dmat, axis=1, keepdims=True))
        qb = q.astype(BF16)
        kb = k.astype(BF16)
        vb = v.astype(BF16)
        qk = lax.dot_general(qb, kb, (((1,), (1,)), ((), ())), preferred_element_type=F32)
        a = jnp.exp(dmat - mt) * qk
        s_in = jnp.exp(inter - mt)
        c = c_ref[dirn]
        qc = jnp.dot(qb, c.astype(BF16), preferred_element_type=F32)
        av = jnp.dot(a.astype(BF16), vb, preferred_element_type=F32)
        num = s_in * qc + av
        den = s_in * jnp.sum(q * n, axis=1, keepdims=True) + jnp.sum(a, axis=1, keepdims=True)
        hs = hs_f if dirn == 0 else hs_b
        hs[pl.ds(r0, L), :] = num / jnp.maximum(jnp.abs(den), jnp.exp(-mt))
        b_last = jnp.sum(f_row, axis=1, keepdims=True)
        g_row = b_last - b_row + i_row
        g_col = b_last - b_col + i_col
        m_new = jnp.maximum(b_last + m, jnp.max(g_row, axis=1, keepdims=True))
        decay = jnp.exp(b_last + m - m_new)
        wk = jnp.exp(g_col - m_new) * k
        c_ref[dirn] = decay * c + lax.dot_general(
            wk.astype(BF16), vb, (((0,), (0,)), ((), ())), preferred_element_type=F32)
        n_new = decay * n + jnp.sum(wk, axis=0, keepdims=True)
        return n_new, m_new

    def body(ci, carry):
        n_f, m_f, n_b, m_b = carry
        n_f, m_f = chunk(0, ci, n_f, m_f)
        n_b, m_b = chunk(1, nc - 1 - ci, n_b, m_b)
        return n_f, m_f, n_b, m_b

    n_f, m_f, n_b, m_b = lax.fori_loop(
        0, nc, body, (n0_ref[0], m0_ref[0, :, 0:1], n0_ref[1], m0_ref[1, :, 0:1]),
        unroll=MLSTM_UNROLL)
    n_ref[0] = n_f
    n_ref[1] = n_b
    m_ref[0] = jnp.broadcast_to(m_f, (1, LANES))
    m_ref[1] = jnp.broadcast_to(m_b, (1, LANES))

    hm = _sigmoid(o_ref[...]) * (hs_f[...] + hs_b[...])
    hc = hm - jnp.mean(hm, axis=1, keepdims=True)
    hn = hc * lax.rsqrt(jnp.mean(hc * hc, axis=1, keepdims=True) + EPS)
    hm_ref[...] = (hn * lng_ref[...] * _silu(z_ref[...])).astype(BF16)


def _mlstm(p, conv_w, conv_b, ln_g, gi, gf, c0, n0, m0):
    B, T, _ = p.shape
    nc = T // CHUNK
    tiles_per_part = M_WIDTH // M_DK

    def col_spec(part):
        return pl.BlockSpec((None, T, M_DK), lambda b, h: (b, 0, part * tiles_per_part + h))

    def par_spec(rows, part):
        return pl.BlockSpec((rows, M_DK), lambda b, h: (0, part * tiles_per_part + h))

    gate_spec = pl.BlockSpec((N_DIR, None, None, nc, CHUNK), lambda b, h: (0, b, h, 0, 0))
    c_spec = pl.BlockSpec((None, N_DIR, None, M_DK, M_DK), lambda b, h: (b, 0, h, 0, 0))
    n_spec = pl.BlockSpec((None, N_DIR, None, 1, M_DK), lambda b, h: (b, 0, h, 0, 0))
    m_spec = pl.BlockSpec((None, N_DIR, None, 1, LANES), lambda b, h: (b, 0, h, 0, 0))
    seq = pltpu.VMEM((T, M_DK), F32)
    return pl.pallas_call(
        _mlstm_kernel,
        grid=(B, M_HEADS),
        in_specs=[col_spec(0), col_spec(1), col_spec(2), col_spec(3), col_spec(4),
                  par_spec(CONV_K, 0), par_spec(CONV_K, 1), par_spec(1, 0), par_spec(1, 1),
                  par_spec(1, 0), gate_spec, gate_spec, c_spec, n_spec, m_spec],
        out_specs=[pl.BlockSpec((None, T, M_DK), lambda b, h: (b, 0, h)), c_spec, n_spec, m_spec],
        out_shape=[jax.ShapeDtypeStruct((B, T, M_WIDTH), BF16),
                   jax.ShapeDtypeStruct((B, N_DIR, M_HEADS, M_DK, M_DK), F32),
                   jax.ShapeDtypeStruct((B, N_DIR, M_HEADS, 1, M_DK), F32),
                   jax.ShapeDtypeStruct((B, N_DIR, M_HEADS, 1, LANES), F32)],
        scratch_shapes=[seq, seq, seq, seq],
        compiler_params=_params(2),
        name="mlstm_scan",
    )(p, p, p, p, p, conv_w, conv_w, conv_b, conv_b, ln_g, gi, gf, c0, n0, m0)


MIX_SLOTS = 3
LORA_SLOTS = 2 * N_DIR
MIX_TT = 512
LORA_TT = 128
MIX_TILE0 = SHIFT_COLS // LANES - 2


def _prep_kernel(x_ref, lo_ref, hi_ref, mu_ref, *rest, grid_shift, n_tok, lora):
    tblk = pl.program_id(0)
    c = pl.program_id(1)
    x = x_ref[...]
    tt, n_b, _ = x.shape
    t = tblk * tt + lax.broadcasted_iota(jnp.int32, x.shape, 0)
    lane4 = lax.broadcasted_iota(jnp.int32, x.shape, 2) % 4
    xe = jnp.concatenate([lo_ref[...], x, hi_ref[...]], axis=0)
    prev = xe[GRID_W - 1:GRID_W - 1 + tt]
    nxt = xe[GRID_W + 1:GRID_W + 1 + tt]
    if grid_shift:
        col = t % GRID_W
        left = jnp.where(col == 0, 0.0, prev)
        right = jnp.where(col == GRID_W - 1, 0.0, nxt)
        up = jnp.where(t < GRID_W, 0.0, xe[:tt])
        down = jnp.where(t >= n_tok - GRID_W, 0.0, xe[2 * GRID_W:])
        sh = jnp.where(lane4 == 0, left, jnp.where(lane4 == 1, right, jnp.where(lane4 == 2, up, down)))
    else:
        sh = jnp.where(lane4 % 2 == 0, jnp.where(t == 0, 0.0, prev), jnp.where(t == n_tok - 1, 0.0, nxt))
    mixed = x + mu_ref[...] * (sh - x)

    def emit(val, dst):
        v4 = val.reshape(tt // 2, 2, n_b, LANES)
        even, odd = v4[:, 0], v4[:, 1]
        if n_b == SUBLANES:
            lo = lax.broadcasted_iota(jnp.int32, even.shape, 2) < R_N
            dst(0, jnp.where(lo, even, pltpu.roll(odd, R_N, 2)))
            dst(1, jnp.where(lo, pltpu.roll(even, R_N, 2), odd))
        else:
            dst(0, even)
            dst(1, odd)

    if not lora:
        (o_ref,) = rest

        def dst(row, val):
            o_ref[:, row] = val

        emit(mixed, dst)
    else:
        w_ref, o_ref = rest
        act = jnp.where(c < N_DIR, jnp.tanh(mixed), mixed)
        prod = jnp.dot(act.reshape(tt * n_b, LANES).astype(BF16), w_ref[...].astype(BF16),
                       preferred_element_type=F32)
        for hp in range(HEAD_PAIRS):
            def dst(row, val, hp=hp):
                o_ref[:, row, hp] = val

            emit(prod[:, hp * LANES:(hp + 1) * LANES].reshape(tt, n_b, LANES), dst)


def _rwkv_prep(p, r_mu, w_lora, grid_shift):
    T, B, _ = p.shape
    n_halo = T // GRID_W

    def specs(tt, col):
        per_blk = tt // GRID_W
        return [pl.BlockSpec((tt, B, LANES), lambda t, c: (t, 0, RS_BLK0 + col(c))),
                pl.BlockSpec((GRID_W, B, LANES),
                             lambda t, c: (jnp.maximum(t * per_blk - 1, 0), 0, RS_BLK0 + col(c))),
                pl.BlockSpec((GRID_W, B, LANES),
                             lambda t, c: (jnp.minimum((t + 1) * per_blk, n_halo - 1), 0, RS_BLK0 + col(c))),
                pl.BlockSpec((1, LANES), lambda t, c: (0, col(c)))]

    tt = min(MIX_TT, T)
    mix = pl.pallas_call(
        functools.partial(_prep_kernel, grid_shift=grid_shift, n_tok=T, lora=False),
        grid=(T // tt, MIX_SLOTS * HEAD_PAIRS),
        in_specs=specs(tt, lambda c: c),
        out_specs=pl.BlockSpec((None, tt // 2, 2, None, B, LANES),
                               lambda t, c: (c // HEAD_PAIRS, t, 0, c % HEAD_PAIRS, 0, 0)),
        out_shape=jax.ShapeDtypeStruct((MIX_SLOTS, T // 2, 2, HEAD_PAIRS, B, LANES), F32),
        compiler_params=_params(2),
        name="rwkv_prep",
    )(p, p, p, r_mu)
    tt = min(LORA_TT, T)
    lora = pl.pallas_call(
        functools.partial(_prep_kernel, grid_shift=grid_shift, n_tok=T, lora=True),
        grid=(T // tt, LORA_SLOTS),
        in_specs=specs(tt, lambda c: MIX_TILE0 + c // N_DIR)
        + [pl.BlockSpec((None, LANES, R_WIDTH), lambda t, c: (c, 0, 0))],
        out_specs=pl.BlockSpec((None, tt // 2, 2, HEAD_PAIRS, B, LANES),
                               lambda t, c: (c, t, 0, 0, 0, 0)),
        out_shape=jax.ShapeDtypeStruct((LORA_SLOTS, T // 2, 2, HEAD_PAIRS, B, LANES), F32),
        compiler_params=_params(2),
        name="rwkv_lora",
    )(p, p, p, r_mu, w_lora)
    return mix, lora


RWKV_TB = 32
J_UNROLL = 16
PREP_UNROLL = 4


def _chain_tiles(ref, pair, group, n_b):
    if n_b == SUBLANES:
        xt = ref[pair].reshape(LANES, LANES).T
        return [xt[:R_N], xt[R_N:]]
    tiles = []
    for parity in range(2):
        xt = ref[pair, parity].reshape(LANES, LANES).T
        tiles.append(jnp.where(group == 0, xt[:R_N], xt[R_N:]))
    return tiles


def _rwkv_kernel(r_ref, k_ref, v_ref, lw_ref, la_ref, pd_ref, ps_ref, s0_ref,
                 y_ref, c_ref, s_ref, kh_s, rh_s, vv_s, kt_s, bt_s, g_s, *, n_b):
    dirn = pl.program_id(0)
    group = pl.program_id(1)
    tb = pl.program_id(2)
    n_tok = y_ref.shape[0]
    n_pairs = n_tok // 2
    fwd = dirn == 0

    @pl.when(tb == 0)
    def _():
        s_ref[...] = s0_ref[...]

    w0, a0 = pd_ref[0], pd_ref[1]
    k_k, k_a, r_k = ps_ref[0], ps_ref[1], ps_ref[2]

    def prep(q, g):
        pair = q + dirn * (n_pairs - 1 - 2 * q)
        tiles = [_chain_tiles(ref, pair, group, n_b) for ref in (r_ref, k_ref, v_ref, lw_ref, la_ref)]
        for i in range(2):
            r, kraw, v, lw, la = (jnp.where(fwd, x[i], x[1 - i]) for x in tiles)
            pos = 2 * q + i
            tok = 2 * pair + i + dirn * (1 - 2 * i)
            wlog = -_softplus(-(w0 + lw)) - 0.5
            a = _sigmoid(a0 + la)
            kk = kraw * k_k
            kk = kk / jnp.maximum(jnp.sqrt(jnp.sum(kk * kk, axis=0, keepdims=True)), 1e-12)
            kz = kraw * (1.0 + (a - 1.0) * k_a)
            c_ref[tok] = jnp.sum(r * kz * r_k, axis=0, keepdims=True) * v
            kh_s[pos] = kk * g
            g = g * jnp.exp(-jnp.exp(wlog))
            g_inv = 1.0 / g
            vv_s[pos] = v
            bt_s[pos] = kk * a * g_inv
            kt_s[pos] = kz * g_inv
            rh_s[pos] = r * g
        return g

    g_s[...] = lax.fori_loop(0, n_pairs, prep, jnp.ones((R_N, LANES), F32), unroll=PREP_UNROLL)

    zeros = jnp.zeros((R_N, LANES), F32)

    def dot_first(jb, u):
        for jj in range(J_UNROLL):
            j = jb * J_UNROLL + jj
            u = u + s_ref[j] * kh_s[0, pl.ds(j, 1), :]
        return u

    def step(pos, u):
        tok = pos + dirn * (n_tok - 1 - 2 * pos)
        nxt = jnp.minimum(pos + 1, n_tok - 1)
        sa = -u
        vt = vv_s[pos]

        def update(jb, carry):
            y, u_next = carry
            for jj in range(J_UNROLL):
                j = jb * J_UNROLL + jj
                sn = s_ref[j] + sa * bt_s[pos, pl.ds(j, 1), :] + vt * kt_s[pos, pl.ds(j, 1), :]
                s_ref[j] = sn
                y = y + sn * rh_s[pos, pl.ds(j, 1), :]
                u_next = u_next + sn * kh_s[nxt, pl.ds(j, 1), :]
            return y, u_next

        y, u_next = lax.fori_loop(0, R_N // J_UNROLL, update, (zeros, zeros))
        y_ref[tok] = y
        return u_next

    lax.fori_loop(0, n_tok, step, lax.fori_loop(0, R_N // J_UNROLL, dot_first, zeros))

    def rescale(jb, carry):
        for jj in range(J_UNROLL):
            j = jb * J_UNROLL + jj
            s_ref[j] = s_ref[j] * g_s[pl.ds(j, 1), :]
        return carry

    lax.fori_loop(0, R_N // J_UNROLL, rescale, 0)


def _rwkv(xs, xl, p_dir, p_sh, s0):
    _, half_t, _, _, B, _ = xs.shape
    T = 2 * half_t
    G = p_sh.shape[0]
    tb = min(RWKV_TB, T)
    nt = T // tb
    rev = lambda dd, t: t + dd * (nt - 1 - 2 * t)

    def slot(s):
        return pl.BlockSpec((None, tb // 2, 2, HEAD_PAIRS, B, LANES),
                            lambda dd, g, t: (s(dd), rev(dd, t), 0, 0, 0, 0))

    state = pl.BlockSpec((None, None, R_N, R_N, LANES), lambda dd, g, t: (dd, g, 0, 0, 0))
    out = pl.BlockSpec((None, tb, R_N, LANES), lambda dd, g, t: (dd, rev(dd, t), 0, g))
    scratch = pltpu.VMEM((tb, R_N, LANES), F32)
    return pl.pallas_call(
        functools.partial(_rwkv_kernel, n_b=B),
        grid=(N_DIR, G, nt),
        in_specs=[slot(lambda dd: 0), slot(lambda dd: 1), slot(lambda dd: 2),
                  slot(lambda dd: dd), slot(lambda dd: N_DIR + dd),
                  pl.BlockSpec((None, None, 2, R_N, LANES), lambda dd, g, t: (dd, g, 0, 0, 0)),
                  pl.BlockSpec((None, 3, R_N, LANES), lambda dd, g, t: (g, 0, 0, 0)),
                  state],
        out_specs=[out, out, state],
        out_shape=[jax.ShapeDtypeStruct((N_DIR, T, R_N, G * LANES), F32),
                   jax.ShapeDtypeStruct((N_DIR, T, R_N, G * LANES), F32),
                   jax.ShapeDtypeStruct((N_DIR, G, R_N, R_N, LANES), F32)],
        scratch_shapes=[scratch] * 5 + [pltpu.VMEM((R_N, LANES), F32)],
        compiler_params=_params(3),
        name="rwkv_scan",
    )(xs, xs, xs, xl, xl, p_dir, p_sh, s0)


def _post_kernel(y_ref, c_ref, z_ref, ln_ref, o_ref, *, n_b):
    n_tok = y_ref.shape[1]
    lo = lax.broadcasted_iota(jnp.int32, (R_N, LANES), 1) < R_N

    def normed(t, g):
        lanes = pl.ds(g * LANES, LANES)
        y = y_ref[0, t, :, lanes] + y_ref[1, t, :, lanes]
        mean = jnp.mean(y, axis=0, keepdims=True)
        yc = y - mean
        var = jnp.mean(yc * yc, axis=0, keepdims=True)
        return (yc * lax.rsqrt(var + LNX_EPS) * ln_ref[0, :, lanes] + ln_ref[1, :, lanes]
                + c_ref[0, t, :, lanes] + c_ref[1, t, :, lanes])

    def gate(t, x):
        z = z_ref[t]
        for hp in range(HEAD_PAIRS):
            o_ref[t, hp] = x[hp * n_b:(hp + 1) * n_b] * _silu(z[:, hp * LANES:(hp + 1) * LANES])

    def body(tp, carry):
        t0 = tp * 2
        if n_b == SUBLANES:
            a, b = normed(t0, 0), normed(t0 + 1, 0)
            xt = jnp.concatenate([jnp.where(lo, a, pltpu.roll(b, R_N, 1)),
                                  jnp.where(lo, pltpu.roll(a, R_N, 1), b)], axis=0)
            x = xt.T
            for i in range(2):
                gate(t0 + i, x[i * R_N:(i + 1) * R_N])
        else:
            for i in range(2):
                gate(t0 + i, jnp.concatenate([normed(t0 + i, 0), normed(t0 + i, 1)], axis=0).T)
        return carry

    lax.fori_loop(0, n_tok // 2, body, 0, unroll=2)


def _rwkv_post(y, c, p, ln):
    _, T, _, CH = y.shape
    B = p.shape[1]
    tb = min(RWKV_TB, T)
    chain = pl.BlockSpec((N_DIR, tb, R_N, CH), lambda t: (0, t, 0, 0))
    nat = pl.BlockSpec((tb, B, R_WIDTH), lambda t: (t, 0, RZ_BLK0))
    return pl.pallas_call(
        functools.partial(_post_kernel, n_b=B),
        grid=(T // tb,),
        in_specs=[chain, chain, nat, pl.BlockSpec((2, R_N, CH), lambda t: (0, 0, 0))],
        out_specs=pl.BlockSpec((tb, HEAD_PAIRS, B, LANES), lambda t: (t, 0, 0, 0)),
        out_shape=jax.ShapeDtypeStruct((T, HEAD_PAIRS, B, LANES), F32),
        compiler_params=_params(1),
        name="rwkv_post",
    )(y, c, p, ln)


OUT_TM = 512


def _out_kernel(hm_ref, yr_ref, w_ref, x_ref, gate_ref, g_ref, o_ref):
    cat = jnp.concatenate([hm_ref[...], yr_ref[...].astype(BF16)], axis=1)
    out = jnp.dot(cat, w_ref[...], preferred_element_type=F32)
    z = x_ref[...] + gate_ref[...] * out
    o_ref[...] = z * lax.rsqrt(jnp.mean(z * z, axis=-1, keepdims=True) + EPS) * g_ref[...]


def _out_proj(hm, yr, w_out, x, gate, final_g):
    B, T, D = x.shape
    tm = min(OUT_TM, T)
    per_b = gate.shape[0] == B
    row = lambda b, t: (b, t, 0)
    return pl.pallas_call(
        _out_kernel,
        grid=(B, T // tm),
        in_specs=[pl.BlockSpec((None, tm, M_WIDTH), row),
                  pl.BlockSpec((None, tm, R_WIDTH), row),
                  pl.BlockSpec((D, D), lambda b, t: (0, 0)),
                  pl.BlockSpec((None, tm, D), row),
                  pl.BlockSpec((None, 1, D), lambda b, t: (b if per_b else 0, 0, 0)),
                  pl.BlockSpec((1, D), lambda b, t: (0, 0))],
        out_specs=pl.BlockSpec((None, tm, D), row),
        out_shape=jax.ShapeDtypeStruct((B, T, D), F32),
        compiler_params=_params(2),
        name="out_proj",
    )(hm, yr, w_out, x, gate, final_g)


def _rmsnorm(x, g):
    return x * lax.rsqrt(jnp.mean(x * x, axis=-1, keepdims=True) + EPS) * g


def _chain_param(p, n_b):
    q = p.reshape(HEAD_PAIRS, 2, R_N)
    if n_b == SUBLANES:
        t = jnp.broadcast_to(q.transpose(2, 1, 0)[..., None], (R_N, 2, HEAD_PAIRS, n_b))
        return t.reshape(1, R_N, LANES)
    t = jnp.broadcast_to(q.transpose(1, 2, 0)[..., None], (2, R_N, HEAD_PAIRS, n_b))
    return t.reshape(2, R_N, LANES)


def _chain_state(s, n_b):
    s6 = s.reshape(n_b, N_DIR, HEAD_PAIRS, 2, R_N, R_N)
    if n_b == SUBLANES:
        return s6.transpose(1, 5, 4, 3, 2, 0).reshape(N_DIR, 1, R_N, R_N, LANES)
    return s6.transpose(1, 3, 5, 4, 2, 0).reshape(N_DIR, 2, R_N, R_N, LANES)


def _unchain_state(s, n_b):
    if n_b == SUBLANES:
        s6 = s.reshape(N_DIR, R_N, R_N, 2, HEAD_PAIRS, n_b).transpose(5, 0, 4, 3, 2, 1)
    else:
        s6 = s.reshape(N_DIR, 2, R_N, R_N, HEAD_PAIRS, n_b).transpose(5, 0, 4, 1, 3, 2)
    return s6.reshape(n_b, N_DIR, R_HEADS, R_N, R_N)


def _block(x, mod, st, lw, norm_g, final_g, grid):
    (w_mproj, w_rproj, m_conv_w, m_conv_b, m_gate_b, m_ln_g, r_mu, r_w0, w_lora, r_a0,
     r_k_k, r_k_a, r_r_k, r_ln_g, r_ln_b, w_out) = lw
    C0, n0, m0, S0 = st
    B, T, D = x.shape
    assert B in (SUBLANES, 2 * SUBLANES), "chain layout is written for 8 or 16 sequences"
    shift, scale, gate = jnp.split(mod, 3, axis=-1)
    h = (_rmsnorm(x, norm_g) * (1.0 + scale) + shift).astype(BF16)
    tm = min(512, B * T)
    p = _mm(h.reshape(B * T, D), w_mproj, tm, M_PROJ_COLS // 2).reshape(B, T, M_PROJ_COLS)
    pr = _mm(h.transpose(1, 0, 2).reshape(T * B, D), w_rproj, tm, R_PROJ_COLS // 2)
    pr = pr.reshape(T, B, R_PROJ_COLS)
    mg = p[..., 5 * M_WIDTH:5 * M_WIDTH + M_GATE_COLS]

    gates = (mg.reshape(B, T, N_DIR, 2, M_HEADS) + m_gate_b).transpose(2, 3, 0, 4, 1)
    nc = T // CHUNK
    log_i = gates[:, 0].reshape(N_DIR, B, M_HEADS, nc, CHUNK)
    log_f = jax.nn.log_sigmoid(gates[:, 1]).reshape(N_DIR, B, M_HEADS, nc, CHUNK)
    hm, Cn, nn, mn = _mlstm(
        p, m_conv_w, m_conv_b.reshape(1, 2 * M_WIDTH), m_ln_g.reshape(1, M_WIDTH), log_i, log_f,
        C0, n0[..., None, :], jnp.broadcast_to(m0[..., None, None], m0.shape + (1, LANES)))

    xs, xl = _rwkv_prep(pr, r_mu.reshape(1, SHIFT_COLS), w_lora, grid)
    p_dir = jnp.stack([jnp.stack([_chain_param(r_w0[z], B), _chain_param(r_a0[z], B)], axis=1)
                       for z in range(N_DIR)])
    p_sh = jnp.stack([_chain_param(r_k_k, B), _chain_param(r_k_a, B),
                      _chain_param(r_r_k.reshape(R_WIDTH), B)], axis=1)
    y_dirs, c_dirs, S_new = _rwkv(xs, xl, p_dir, p_sh, _chain_state(S0, B))
    G = p_sh.shape[0]
    ln = jnp.stack([_chain_param(r_ln_g, B), _chain_param(r_ln_b, B)])
    ln = ln.transpose(0, 2, 1, 3).reshape(2, R_N, G * LANES)
    yr = _rwkv_post(y_dirs, c_dirs, pr, ln)
    yr = yr.transpose(2, 0, 1, 3).reshape(B, T, R_WIDTH)

    out = _out_proj(hm, yr, w_out, x, gate, final_g.reshape(1, D))
    return out, (Cn, nn[..., 0, :], mn[..., 0, 0], _unchain_state(S_new, B))


def kernel(x_prompt, x_sample, state_mlstm_C, state_mlstm_n, state_mlstm_m, state_rwkv_S, c, c_ctx,
           norm_g, w_ada, b_ada, w_in, m_conv_w, m_conv_b, m_gate_b, m_ln_g, r_mu, r_w0, r_w2, r_a0,
           r_a2, r_k_k, r_k_a, r_r_k, r_ln_g, r_ln_b, w_out, final_g):
    depth = w_in.shape[0]
    assert depth == 1, "the final RMSNorm is fused into the single layer's output projection"
    bp = x_prompt.shape[0]
    ctx_state0 = (jnp.zeros((bp, N_DIR, M_HEADS, M_DK, M_DK), F32),
                  jnp.zeros((bp, N_DIR, M_HEADS, M_DK), F32),
                  jnp.full((bp, N_DIR, M_HEADS), -jnp.inf, F32),
                  jnp.zeros((bp, N_DIR, R_HEADS, R_N, R_N), F32))
    l = 0
    w_l = w_in[l]
    gate_lo = 5 * M_WIDTH
    gate_hi = gate_lo + M_GATE_COLS
    w_mproj = jnp.concatenate(
        [w_l[:, :gate_hi], jnp.zeros((D_MODEL, GATE_PAD - M_GATE_COLS), F32)], axis=1).astype(BF16)
    w_rproj = w_l[:, gate_hi:].astype(BF16)
    zpad = jnp.zeros((LORA, R_WIDTH), F32)
    w_lora = jnp.stack([jnp.concatenate([r_w2[l, 0], zpad]), jnp.concatenate([zpad, r_w2[l, 1]]),
                        jnp.concatenate([r_a2[l, 0], zpad]), jnp.concatenate([zpad, r_a2[l, 1]])])
    lw = (w_mproj, w_rproj, m_conv_w[l], m_conv_b[l], m_gate_b[l], m_ln_g[l], r_mu[l], r_w0[l], w_lora,
          r_a0[l], r_k_k[l], r_k_a[l], r_r_k[l], r_ln_g[l], r_ln_b[l], w_out[l].astype(BF16))
    n_cond = 1 + c.shape[0]
    cond = jnp.concatenate([c_ctx[None, :], c, jnp.zeros((16 - n_cond, D_MODEL), F32)], axis=0)
    mod = _mm(jax.nn.silu(cond), w_ada[l], 16, 1536)[:n_cond] + b_ada[l]
    mod_p = mod[0:1, None, :]
    mod_s = mod[1:, None, :]
    y_prompt, (Cp, np_, mp, Sp) = _block(x_prompt, mod_p, ctx_state0, lw, norm_g[l], final_g, False)
    st_s = (state_mlstm_C[:, l], state_mlstm_n[:, l], state_mlstm_m[:, l], state_rwkv_S[:, l])
    y_sample, _ = _block(x_sample, mod_s, st_s, lw, norm_g[l], final_g, True)
    return (y_prompt, y_sample, Cp[:, None], np_[:, None], mp[:, None], Sp[:, None])
```

```python
import functools

import jax
import jax.numpy as jnp
import numpy as np
from jax import lax
from jax.experimental import pallas as pl
from jax.experimental.pallas import tpu as pltpu

D_MODEL = 2048
GRID_W = 64
N_DIR = 2
M_WIDTH = D_MODEL // 2
M_HEADS = 4
M_DK = M_WIDTH // M_HEADS
R_WIDTH = D_MODEL - M_WIDTH
R_N = 64
R_HEADS = R_WIDTH // R_N
LORA = 64
CONV_K = 3
CHUNK = 64
EPS = 1e-6
LNX_EPS = 64e-5
M_GATE_COLS = N_DIR * 2 * M_HEADS
SHIFT_COLS = 3 * R_WIDTH + 2 * N_DIR * LORA

LANES = 128
SUBLANES = 8
HEAD_PAIRS = R_WIDTH // LANES
GATE_PAD = 2 * LANES
M_PROJ_COLS = 5 * M_WIDTH + GATE_PAD
R_PROJ_COLS = R_WIDTH + SHIFT_COLS
RZ_BLK0 = 0
RS_BLK0 = R_WIDTH // LANES
VMEM_LIMIT = 48 * 1024 * 1024

F32 = jnp.float32
BF16 = jnp.bfloat16


def _params(n_axes, vmem_limit=VMEM_LIMIT):
    return pltpu.CompilerParams(dimension_semantics=("arbitrary",) * n_axes,
                                vmem_limit_bytes=vmem_limit)


def _sigmoid(x):
    return 1.0 / (1.0 + jnp.exp(-x))


def _silu(x):
    return x * _sigmoid(x)


def _softplus(x):
    return jnp.maximum(x, 0.0) + jnp.log(1.0 + jnp.exp(-jnp.abs(x)))


def _mm_kernel(x_ref, w_ref, o_ref):
    o_ref[...] = jnp.dot(x_ref[...].astype(BF16), w_ref[...].astype(BF16),
                         preferred_element_type=F32)


def _mm(x, w, tm, tn):
    M, K = x.shape
    N = w.shape[1]
    assert M % tm == 0 and N % tn == 0
    return pl.pallas_call(
        _mm_kernel,
        grid=(N // tn, M // tm),
        in_specs=[pl.BlockSpec((tm, K), lambda j, i: (i, 0)),
                  pl.BlockSpec((K, tn), lambda j, i: (0, j))],
        out_specs=pl.BlockSpec((tm, tn), lambda j, i: (i, j)),
        out_shape=jax.ShapeDtypeStruct((M, N), F32),
        compiler_params=_params(2),
        name="proj_mm",
    )(x, w)


MLSTM_UNROLL = 4
MLSTM_VMEM_LIMIT = 56 * 1024 * 1024


def _mlstm_kernel(q_ref, k_ref, v_ref, o_ref, z_ref, wq_ref, wk_ref, bq_ref, bk_ref, lng_ref,
                  gi_ref, gf_ref, c0_ref, n0_ref, m0_ref,
                  hm_ref, c_ref, n_ref, m_ref, qs, ks, hs_f, hs_b, bcol_s, icol_s, rmax_s, rows_s):
    n_tok = q_ref.shape[0]
    nc = n_tok // CHUNK
    L = CHUNK
    tpos = lax.broadcasted_iota(jnp.int32, (n_tok, M_DK), 0)

    def conv_silu(x_ref, w_ref, b_ref):
        x = x_ref[...]
        prev = jnp.where(tpos == 0, 0.0, pltpu.roll(x, 1, 0))
        nxt = jnp.where(tpos == n_tok - 1, 0.0, pltpu.roll(x, n_tok - 1, 0))
        return _silu(prev * w_ref[0:1, :] + x * w_ref[1:2, :] + nxt * w_ref[2:3, :] + b_ref[...])

    qs[...] = conv_silu(q_ref, wq_ref, bq_ref)
    ks[...] = conv_silu(k_ref, wk_ref, bk_ref) * (M_DK ** -0.5)
    c_ref[...] = c0_ref[...]

    row = lax.broadcasted_iota(jnp.int32, (L, L), 0)
    col = lax.broadcasted_iota(jnp.int32, (L, L), 1)
    eye = row == col
    neg_inf = jnp.float32(-jnp.inf)

    def masks(dirn):
        seen = col <= row if dirn == 0 else col >= row
        seen_t = row <= col if dirn == 0 else row >= col
        return seen, seen_t

    def gate_stats(dirn, cc):
        seen, seen_t = masks(dirn)
        idx = dirn * nc + cc
        f_row = gf_ref[dirn, pl.ds(cc, 1), :]
        i_row = gi_ref[dirn, pl.ds(cc, 1), :]
        f_col = jnp.sum(jnp.where(eye, f_row, 0.0), axis=1, keepdims=True)
        i_col = jnp.sum(jnp.where(eye, i_row, 0.0), axis=1, keepdims=True)
        b_col = jnp.sum(jnp.where(seen, f_row, 0.0), axis=1, keepdims=True)
        b_row = jnp.sum(jnp.where(seen_t, f_col, 0.0), axis=0, keepdims=True)
        dmat = jnp.where(seen, b_col - b_row + i_row, neg_inf)
        b_last = jnp.sum(f_row, axis=1, keepdims=True)
        g_row = b_last - b_row + i_row
        bcol_s[idx] = jnp.broadcast_to(b_col, (L, LANES))
        icol_s[idx] = jnp.broadcast_to(i_col, (L, LANES))
        rmax_s[idx] = jnp.broadcast_to(jnp.max(dmat, axis=1, keepdims=True), (L, LANES))
        rows_s[idx, 0:1, 0:L] = b_row
        rows_s[idx, 1:2, 0:L] = i_row
        rows_s[idx, 2:3, :] = jnp.broadcast_to(b_last, (1, LANES))
        rows_s[idx, 3:4, :] = jnp.broadcast_to(jnp.max(g_row, axis=1, keepdims=True), (1, LANES))

    def stats_body(cc, carry):
        gate_stats(0, cc)
        gate_stats(1, cc)
        return carry

    lax.fori_loop(0, nc, stats_body, 0, unroll=min(4, nc))

    def wide(x):
        return jnp.concatenate([x] * (M_DK // LANES), axis=1)

    def chunk(dirn, cc, n, m):
        seen, _ = masks(dirn)
        idx = dirn * nc + cc
        r0 = pl.multiple_of(cc * L, L)
        q = qs[pl.ds(r0, L), :]
        k = ks[pl.ds(r0, L), :]
        v = v_ref[pl.ds(r0, L), :]
        b_col = bcol_s[idx]
        b_row = rows_s[idx, 0:1, 0:L]
        i_row = rows_s[idx, 1:2, 0:L]
        b_last = rows_s[idx, 2:3, :]
        inter = b_col + m
        mt = jnp.maximum(inter, rmax_s[idx])
        dmat = jnp.where(seen, b_col[:, :L] - b_row + i_row, neg_inf)
        qb = q.astype(BF16)
        kb = k.astype(BF16)
        vb = v.astype(BF16)
        qk = lax.dot_general(qb, kb, (((1,), (1,)), ((), ())), preferred_element_type=F32)
        a = jnp.exp(dmat - mt[:, :L]) * qk
        s_in = jnp.exp(inter - mt)
        c = c_ref[dirn]
        qc = jnp.dot(qb, c.astype(BF16), preferred_element_type=F32)
        av = jnp.dot(a.astype(BF16), vb, preferred_element_type=F32)
        num = wide(s_in) * qc + av
        den = s_in * jnp.sum(q * n, axis=1, keepdims=True) + jnp.sum(a, axis=1, keepdims=True)
        hs = hs_f if dirn == 0 else hs_b
        hs[pl.ds(r0, L), :] = num / wide(jnp.maximum(jnp.abs(den), jnp.exp(-mt)))
        m_new = jnp.maximum(b_last + m, rows_s[idx, 3:4, :])
        decay = wide(jnp.exp(b_last + m - m_new))
        g_col = b_last - b_col + icol_s[idx]
        wk = wide(jnp.exp(g_col - m_new)) * k
        c_ref[dirn] = decay * c + lax.dot_general(
            wk.astype(BF16), vb, (((0,), (0,)), ((), ())), preferred_element_type=F32)
        n_new = decay * n + jnp.sum(wk, axis=0, keepdims=True)
        return n_new, m_new

    def body(ci, carry):
        n_f, m_f, n_b, m_b = carry
        n_f, m_f = chunk(0, ci, n_f, m_f)
        n_b, m_b = chunk(1, nc - 1 - ci, n_b, m_b)
        return n_f, m_f, n_b, m_b

    n_f, m_f, n_b, m_b = lax.fori_loop(
        0, nc, body, (n0_ref[0], m0_ref[0], n0_ref[1], m0_ref[1]), unroll=MLSTM_UNROLL)
    n_ref[0] = n_f
    n_ref[1] = n_b
    m_ref[0] = m_f
    m_ref[1] = m_b

    hm = _sigmoid(o_ref[...]) * (hs_f[...] + hs_b[...])
    hc = hm - jnp.mean(hm, axis=1, keepdims=True)
    hn = hc * lax.rsqrt(jnp.mean(hc * hc, axis=1, keepdims=True) + EPS)
    hm_ref[...] = (hn * lng_ref[...] * _silu(z_ref[...])).astype(BF16)


def _mlstm(p, conv_w, conv_b, ln_g, gi, gf, c0, n0, m0):
    B, T, _ = p.shape
    nc = T // CHUNK
    tiles_per_part = M_WIDTH // M_DK

    def col_spec(part):
        return pl.BlockSpec((None, T, M_DK), lambda b, h: (b, 0, part * tiles_per_part + h))

    def par_spec(rows, part):
        return pl.BlockSpec((rows, M_DK), lambda b, h: (0, part * tiles_per_part + h))

    gate_spec = pl.BlockSpec((N_DIR, None, None, nc, CHUNK), lambda b, h: (0, b, h, 0, 0))
    c_spec = pl.BlockSpec((None, N_DIR, None, M_DK, M_DK), lambda b, h: (b, 0, h, 0, 0))
    n_spec = pl.BlockSpec((None, N_DIR, None, 1, M_DK), lambda b, h: (b, 0, h, 0, 0))
    m_spec = pl.BlockSpec((None, N_DIR, None, 1, LANES), lambda b, h: (b, 0, h, 0, 0))
    seq = pltpu.VMEM((T, M_DK), F32)
    stat = pltpu.VMEM((N_DIR * nc, CHUNK, LANES), F32)
    return pl.pallas_call(
        _mlstm_kernel,
        grid=(B, M_HEADS),
        in_specs=[col_spec(0), col_spec(1), col_spec(2), col_spec(3), col_spec(4),
                  par_spec(CONV_K, 0), par_spec(CONV_K, 1), par_spec(1, 0), par_spec(1, 1),
                  par_spec(1, 0), gate_spec, gate_spec, c_spec, n_spec, m_spec],
        out_specs=[pl.BlockSpec((None, T, M_DK), lambda b, h: (b, 0, h)), c_spec, n_spec, m_spec],
        out_shape=[jax.ShapeDtypeStruct((B, T, M_WIDTH), BF16),
                   jax.ShapeDtypeStruct((B, N_DIR, M_HEADS, M_DK, M_DK), F32),
                   jax.ShapeDtypeStruct((B, N_DIR, M_HEADS, 1, M_DK), F32),
                   jax.ShapeDtypeStruct((B, N_DIR, M_HEADS, 1, LANES), F32)],
        scratch_shapes=[seq, seq, seq, seq, stat, stat, stat,
                        pltpu.VMEM((N_DIR * nc, SUBLANES, LANES), F32)],
        compiler_params=_params(2, MLSTM_VMEM_LIMIT),
        name="mlstm_scan",
    )(p, p, p, p, p, conv_w, conv_w, conv_b, conv_b, ln_g, gi, gf, c0, n0, m0)


MIX_SLOTS = 3
LORA_SLOTS = 2 * N_DIR
MIX_TT = 512
LORA_TT = 128
MIX_TILE0 = SHIFT_COLS // LANES - 2


def _prep_kernel(x_ref, lo_ref, hi_ref, mu_ref, *rest, grid_shift, n_tok, lora):
    tblk = pl.program_id(0)
    c = pl.program_id(1)
    x = x_ref[...]
    tt, n_b, _ = x.shape
    t = tblk * tt + lax.broadcasted_iota(jnp.int32, x.shape, 0)
    lane4 = lax.broadcasted_iota(jnp.int32, x.shape, 2) % 4
    xe = jnp.concatenate([lo_ref[...], x, hi_ref[...]], axis=0)
    prev = xe[GRID_W - 1:GRID_W - 1 + tt]
    nxt = xe[GRID_W + 1:GRID_W + 1 + tt]
    if grid_shift:
        col = t % GRID_W
        left = jnp.where(col == 0, 0.0, prev)
        right = jnp.where(col == GRID_W - 1, 0.0, nxt)
        up = jnp.where(t < GRID_W, 0.0, xe[:tt])
        down = jnp.where(t >= n_tok - GRID_W, 0.0, xe[2 * GRID_W:])
        sh = jnp.where(lane4 == 0, left, jnp.where(lane4 == 1, right, jnp.where(lane4 == 2, up, down)))
    else:
        sh = jnp.where(lane4 % 2 == 0, jnp.where(t == 0, 0.0, prev), jnp.where(t == n_tok - 1, 0.0, nxt))
    mixed = x + mu_ref[...] * (sh - x)

    def emit(val, dst):
        v4 = val.reshape(tt // 2, 2, n_b, LANES)
        even, odd = v4[:, 0], v4[:, 1]
        if n_b == SUBLANES:
            lo = lax.broadcasted_iota(jnp.int32, even.shape, 2) < R_N
            dst(0, jnp.where(lo, even, pltpu.roll(odd, R_N, 2)))
            dst(1, jnp.where(lo, pltpu.roll(even, R_N, 2), odd))
        else:
            dst(0, even)
            dst(1, odd)

    if not lora:
        (o_ref,) = rest

        def dst(row, val):
            o_ref[:, row] = val

        emit(mixed, dst)
    else:
        w_ref, o_ref = rest
        act = jnp.where(c < N_DIR, jnp.tanh(mixed), mixed)
        prod = jnp.dot(act.reshape(tt * n_b, LANES).astype(BF16), w_ref[...].astype(BF16),
                       preferred_element_type=F32)
        for hp in range(HEAD_PAIRS):
            def dst(row, val, hp=hp):
                o_ref[:, row, hp] = val

            emit(prod[:, hp * LANES:(hp + 1) * LANES].reshape(tt, n_b, LANES), dst)


def _rwkv_prep(p, r_mu, w_lora, grid_shift):
    T, B, _ = p.shape
    n_halo = T // GRID_W

    def specs(tt, col):
        per_blk = tt // GRID_W
        return [pl.BlockSpec((tt, B, LANES), lambda t, c: (t, 0, RS_BLK0 + col(c))),
                pl.BlockSpec((GRID_W, B, LANES),
                             lambda t, c: (jnp.maximum(t * per_blk - 1, 0), 0, RS_BLK0 + col(c))),
                pl.BlockSpec((GRID_W, B, LANES),
                             lambda t, c: (jnp.minimum((t + 1) * per_blk, n_halo - 1), 0, RS_BLK0 + col(c))),
                pl.BlockSpec((1, LANES), lambda t, c: (0, col(c)))]

    tt = min(MIX_TT, T)
    mix = pl.pallas_call(
        functools.partial(_prep_kernel, grid_shift=grid_shift, n_tok=T, lora=False),
        grid=(T // tt, MIX_SLOTS * HEAD_PAIRS),
        in_specs=specs(tt, lambda c: c),
        out_specs=pl.BlockSpec((None, tt // 2, 2, None, B, LANES),
                               lambda t, c: (c // HEAD_PAIRS, t, 0, c % HEAD_PAIRS, 0, 0)),
        out_shape=jax.ShapeDtypeStruct((MIX_SLOTS, T // 2, 2, HEAD_PAIRS, B, LANES), F32),
        compiler_params=_params(2),
        name="rwkv_prep",
    )(p, p, p, r_mu)
    tt = min(LORA_TT, T)
    lora = pl.pallas_call(
        functools.partial(_prep_kernel, grid_shift=grid_shift, n_tok=T, lora=True),
        grid=(T // tt, LORA_SLOTS),
        in_specs=specs(tt, lambda c: MIX_TILE0 + c // N_DIR)
        + [pl.BlockSpec((None, LANES, R_WIDTH), lambda t, c: (c, 0, 0))],
        out_specs=pl.BlockSpec((None, tt // 2, 2, HEAD_PAIRS, B, LANES),
                               lambda t, c: (c, t, 0, 0, 0, 0)),
        out_shape=jax.ShapeDtypeStruct((LORA_SLOTS, T // 2, 2, HEAD_PAIRS, B, LANES), F32),
        compiler_params=_params(2),
        name="rwkv_lora",
    )(p, p, p, r_mu, w_lora)
    return mix, lora


RWKV_TB = 32
J_UNROLL = 16
PREP_UNROLL = 4


def _chain_tiles(ref, pair, group, n_b):
    if n_b == SUBLANES:
        xt = ref[pair].reshape(LANES, LANES).T
        return [xt[:R_N], xt[R_N:]]
    tiles = []
    for parity in range(2):
        xt = ref[pair, parity].reshape(LANES, LANES).T
        tiles.append(jnp.where(group == 0, xt[:R_N], xt[R_N:]))
    return tiles


def _rwkv_kernel(r_ref, k_ref, v_ref, lw_ref, la_ref, pd_ref, ps_ref, s0_ref,
                 y_ref, c_ref, s_ref, kh_s, rh_s, vv_s, kt_s, bt_s, g_s, *, n_b):
    dirn = pl.program_id(0)
    group = pl.program_id(1)
    tb = pl.program_id(2)
    n_tok = y_ref.shape[0]
    n_pairs = n_tok // 2
    fwd = dirn == 0

    @pl.when(tb == 0)
    def _():
        s_ref[...] = s0_ref[...]

    w0, a0 = pd_ref[0], pd_ref[1]
    k_k, k_a, r_k = ps_ref[0], ps_ref[1], ps_ref[2]

    def prep(q, g):
        pair = q + dirn * (n_pairs - 1 - 2 * q)
        tiles = [_chain_tiles(ref, pair, group, n_b) for ref in (r_ref, k_ref, v_ref, lw_ref, la_ref)]
        for i in range(2):
            r, kraw, v, lw, la = (jnp.where(fwd, x[i], x[1 - i]) for x in tiles)
            pos = 2 * q + i
            tok = 2 * pair + i + dirn * (1 - 2 * i)
            wlog = -_softplus(-(w0 + lw)) - 0.5
            a = _sigmoid(a0 + la)
            kk = kraw * k_k
            kk = kk / jnp.maximum(jnp.sqrt(jnp.sum(kk * kk, axis=0, keepdims=True)), 1e-12)
            kz = kraw * (1.0 + (a - 1.0) * k_a)
            c_ref[tok] = jnp.sum(r * kz * r_k, axis=0, keepdims=True) * v
            kh_s[pos] = kk * g
            g = g * jnp.exp(-jnp.exp(wlog))
            g_inv = 1.0 / g
            vv_s[pos] = v
            bt_s[pos] = kk * a * g_inv
            kt_s[pos] = kz * g_inv
            rh_s[pos] = r * g
        return g

    g_s[...] = lax.fori_loop(0, n_pairs, prep, jnp.ones((R_N, LANES), F32), unroll=PREP_UNROLL)

    zeros = jnp.zeros((R_N, LANES), F32)

    def dot_first(jb, u):
        for jj in range(J_UNROLL):
            j = jb * J_UNROLL + jj
            u = u + s_ref[j] * kh_s[0, pl.ds(j, 1), :]
        return u

    def step(pos, u):
        tok = pos + dirn * (n_tok - 1 - 2 * pos)
        nxt = jnp.minimum(pos + 1, n_tok - 1)
        sa = -u
        vt = vv_s[pos]

        def update(jb, carry):
            y, u_next = carry
            for jj in range(J_UNROLL):
                j = jb * J_UNROLL + jj
                sn = s_ref[j] + sa * bt_s[pos, pl.ds(j, 1), :] + vt * kt_s[pos, pl.ds(j, 1), :]
                s_ref[j] = sn
                y = y + sn * rh_s[pos, pl.ds(j, 1), :]
                u_next = u_next + sn * kh_s[nxt, pl.ds(j, 1), :]
            return y, u_next

        y, u_next = lax.fori_loop(0, R_N // J_UNROLL, update, (zeros, zeros))
        y_ref[tok] = y
        return u_next

    lax.fori_loop(0, n_tok, step, lax.fori_loop(0, R_N // J_UNROLL, dot_first, zeros))

    def rescale(jb, carry):
        for jj in range(J_UNROLL):
            j = jb * J_UNROLL + jj
            s_ref[j] = s_ref[j] * g_s[pl.ds(j, 1), :]
        return carry

    lax.fori_loop(0, R_N // J_UNROLL, rescale, 0)


def _rwkv(xs, xl, p_dir, p_sh, s0):
    _, half_t, _, _, B, _ = xs.shape
    T = 2 * half_t
    G = p_sh.shape[0]
    tb = min(RWKV_TB, T)
    nt = T // tb
    rev = lambda dd, t: t + dd * (nt - 1 - 2 * t)

    def slot(s):
        return pl.BlockSpec((None, tb // 2, 2, HEAD_PAIRS, B, LANES),
                            lambda dd, g, t: (s(dd), rev(dd, t), 0, 0, 0, 0))

    state = pl.BlockSpec((None, None, R_N, R_N, LANES), lambda dd, g, t: (dd, g, 0, 0, 0))
    out = pl.BlockSpec((None, tb, R_N, LANES), lambda dd, g, t: (dd, rev(dd, t), 0, g))
    scratch = pltpu.VMEM((tb, R_N, LANES), F32)
    return pl.pallas_call(
        functools.partial(_rwkv_kernel, n_b=B),
        grid=(N_DIR, G, nt),
        in_specs=[slot(lambda dd: 0), slot(lambda dd: 1), slot(lambda dd: 2),
                  slot(lambda dd: dd), slot(lambda dd: N_DIR + dd),
                  pl.BlockSpec((None, None, 2, R_N, LANES), lambda dd, g, t: (dd, g, 0, 0, 0)),
                  pl.BlockSpec((None, 3, R_N, LANES), lambda dd, g, t: (g, 0, 0, 0)),
                  state],
        out_specs=[out, out, state],
        out_shape=[jax.ShapeDtypeStruct((N_DIR, T, R_N, G * LANES), F32),
                   jax.ShapeDtypeStruct((N_DIR, T, R_N, G * LANES), F32),
                   jax.ShapeDtypeStruct((N_DIR, G, R_N, R_N, LANES), F32)],
        scratch_shapes=[scratch] * 5 + [pltpu.VMEM((R_N, LANES), F32)],
        compiler_params=_params(3),
        name="rwkv_scan",
    )(xs, xs, xs, xl, xl, p_dir, p_sh, s0)


def _post_kernel(y_ref, c_ref, z_ref, ln_ref, o_ref, *, n_b):
    n_tok = y_ref.shape[1]
    lo = lax.broadcasted_iota(jnp.int32, (R_N, LANES), 1) < R_N

    def normed(t, g):
        lanes = pl.ds(g * LANES, LANES)
        y = y_ref[0, t, :, lanes] + y_ref[1, t, :, lanes]
        mean = jnp.mean(y, axis=0, keepdims=True)
        yc = y - mean
        var = jnp.mean(yc * yc, axis=0, keepdims=True)
        return (yc * lax.rsqrt(var + LNX_EPS) * ln_ref[0, :, lanes] + ln_ref[1, :, lanes]
                + c_ref[0, t, :, lanes] + c_ref[1, t, :, lanes])

    def gate(t, x):
        z = z_ref[t]
        for hp in range(HEAD_PAIRS):
            o_ref[t, hp] = x[hp * n_b:(hp + 1) * n_b] * _silu(z[:, hp * LANES:(hp + 1) * LANES])

    def body(tp, carry):
        t0 = tp * 2
        if n_b == SUBLANES:
            a, b = normed(t0, 0), normed(t0 + 1, 0)
            xt = jnp.concatenate([jnp.where(lo, a, pltpu.roll(b, R_N, 1)),
                                  jnp.where(lo, pltpu.roll(a, R_N, 1), b)], axis=0)
            x = xt.T
            for i in range(2):
                gate(t0 + i, x[i * R_N:(i + 1) * R_N])
        else:
            for i in range(2):
                gate(t0 + i, jnp.concatenate([normed(t0 + i, 0), normed(t0 + i, 1)], axis=0).T)
        return carry

    lax.fori_loop(0, n_tok // 2, body, 0, unroll=2)


def _rwkv_post(y, c, p, ln):
    _, T, _, CH = y.shape
    B = p.shape[1]
    tb = min(RWKV_TB, T)
    chain = pl.BlockSpec((N_DIR, tb, R_N, CH), lambda t: (0, t, 0, 0))
    nat = pl.BlockSpec((tb, B, R_WIDTH), lambda t: (t, 0, RZ_BLK0))
    return pl.pallas_call(
        functools.partial(_post_kernel, n_b=B),
        grid=(T // tb,),
        in_specs=[chain, chain, nat, pl.BlockSpec((2, R_N, CH), lambda t: (0, 0, 0))],
        out_specs=pl.BlockSpec((tb, HEAD_PAIRS, B, LANES), lambda t: (t, 0, 0, 0)),
        out_shape=jax.ShapeDtypeStruct((T, HEAD_PAIRS, B, LANES), F32),
        compiler_params=_params(1),
        name="rwkv_post",
    )(y, c, p, ln)


OUT_TM = 512


def _out_kernel(hm_ref, yr_ref, w_ref, x_ref, gate_ref, g_ref, o_ref):
    cat = jnp.concatenate([hm_ref[...], yr_ref[...].astype(BF16)], axis=1)
    out = jnp.dot(cat, w_ref[...], preferred_element_type=F32)
    z = x_ref[...] + gate_ref[...] * out
    o_ref[...] = z * lax.rsqrt(jnp.mean(z * z, axis=-1, keepdims=True) + EPS) * g_ref[...]


def _out_proj(hm, yr, w_out, x, gate, final_g):
    B, T, D = x.shape
    tm = min(OUT_TM, T)
    per_b = gate.shape[0] == B
    row = lambda b, t: (b, t, 0)
    return pl.pallas_call(
        _out_kernel,
        grid=(B, T // tm),
        in_specs=[pl.BlockSpec((None, tm, M_WIDTH), row),
                  pl.BlockSpec((None, tm, R_WIDTH), row),
                  pl.BlockSpec((D, D), lambda b, t: (0, 0)),
                  pl.BlockSpec((None, tm, D), row),
                  pl.BlockSpec((None, 1, D), lambda b, t: (b if per_b else 0, 0, 0)),
                  pl.BlockSpec((1, D), lambda b, t: (0, 0))],
        out_specs=pl.BlockSpec((None, tm, D), row),
        out_shape=jax.ShapeDtypeStruct((B, T, D), F32),
        compiler_params=_params(2),
        name="out_proj",
    )(hm, yr, w_out, x, gate, final_g)


def _rmsnorm(x, g):
    return x * lax.rsqrt(jnp.mean(x * x, axis=-1, keepdims=True) + EPS) * g


def _chain_param(p, n_b):
    q = p.reshape(HEAD_PAIRS, 2, R_N)
    if n_b == SUBLANES:
        t = jnp.broadcast_to(q.transpose(2, 1, 0)[..., None], (R_N, 2, HEAD_PAIRS, n_b))
        return t.reshape(1, R_N, LANES)
    t = jnp.broadcast_to(q.transpose(1, 2, 0)[..., None], (2, R_N, HEAD_PAIRS, n_b))
    return t.reshape(2, R_N, LANES)


def _chain_state(s, n_b):
    s6 = s.reshape(n_b, N_DIR, HEAD_PAIRS, 2, R_N, R_N)
    if n_b == SUBLANES:
        return s6.transpose(1, 5, 4, 3, 2, 0).reshape(N_DIR, 1, R_N, R_N, LANES)
    return s6.transpose(1, 3, 5, 4, 2, 0).reshape(N_DIR, 2, R_N, R_N, LANES)


def _unchain_state(s, n_b):
    if n_b == SUBLANES:
        s6 = s.reshape(N_DIR, R_N, R_N, 2, HEAD_PAIRS, n_b).transpose(5, 0, 4, 3, 2, 1)
    else:
        s6 = s.reshape(N_DIR, 2, R_N, R_N, HEAD_PAIRS, n_b).transpose(5, 0, 4, 1, 3, 2)
    return s6.reshape(n_b, N_DIR, R_HEADS, R_N, R_N)


def _block(x, mod, st, lw, norm_g, final_g, grid):
    (w_mproj, w_rproj, m_conv_w, m_conv_b, m_gate_b, m_ln_g, r_mu, r_w0, w_lora, r_a0,
     r_k_k, r_k_a, r_r_k, r_ln_g, r_ln_b, w_out) = lw
    C0, n0, m0, S0 = st
    B, T, D = x.shape
    assert B in (SUBLANES, 2 * SUBLANES), "chain layout is written for 8 or 16 sequences"
    shift, scale, gate = jnp.split(mod, 3, axis=-1)
    h = (_rmsnorm(x, norm_g) * (1.0 + scale) + shift).astype(BF16)
    tm = min(512, B * T)
    p = _mm(h.reshape(B * T, D), w_mproj, tm, M_PROJ_COLS // 2).reshape(B, T, M_PROJ_COLS)
    pr = _mm(h.transpose(1, 0, 2).reshape(T * B, D), w_rproj, tm, R_PROJ_COLS // 2)
    pr = pr.reshape(T, B, R_PROJ_COLS)
    mg = p[..., 5 * M_WIDTH:5 * M_WIDTH + M_GATE_COLS]

    gates = (mg.reshape(B, T, N_DIR, 2, M_HEADS) + m_gate_b).transpose(2, 3, 0, 4, 1)
    nc = T // CHUNK
    log_i = gates[:, 0].reshape(N_DIR, B, M_HEADS, nc, CHUNK)
    log_f = jax.nn.log_sigmoid(gates[:, 1]).reshape(N_DIR, B, M_HEADS, nc, CHUNK)
    hm, Cn, nn, mn = _mlstm(
        p, m_conv_w, m_conv_b.reshape(1, 2 * M_WIDTH), m_ln_g.reshape(1, M_WIDTH), log_i, log_f,
        C0, n0[..., None, :], jnp.broadcast_to(m0[..., None, None], m0.shape + (1, LANES)))

    xs, xl = _rwkv_prep(pr, r_mu.reshape(1, SHIFT_COLS), w_lora, grid)
    p_dir = jnp.stack([jnp.stack([_chain_param(r_w0[z], B), _chain_param(r_a0[z], B)], axis=1)
                       for z in range(N_DIR)])
    p_sh = jnp.stack([_chain_param(r_k_k, B), _chain_param(r_k_a, B),
                      _chain_param(r_r_k.reshape(R_WIDTH), B)], axis=1)
    y_dirs, c_dirs, S_new = _rwkv(xs, xl, p_dir, p_sh, _chain_state(S0, B))
    G = p_sh.shape[0]
    ln = jnp.stack([_chain_param(r_ln_g, B), _chain_param(r_ln_b, B)])
    ln = ln.transpose(0, 2, 1, 3).reshape(2, R_N, G * LANES)
    yr = _rwkv_post(y_dirs, c_dirs, pr, ln)
    yr = yr.transpose(2, 0, 1, 3).reshape(B, T, R_WIDTH)

    out = _out_proj(hm, yr, w_out, x, gate, final_g.reshape(1, D))
    return out, (Cn, nn[..., 0, :], mn[..., 0, 0], _unchain_state(S_new, B))


def kernel(x_prompt, x_sample, state_mlstm_C, state_mlstm_n, state_mlstm_m, state_rwkv_S, c, c_ctx,
           norm_g, w_ada, b_ada, w_in, m_conv_w, m_conv_b, m_gate_b, m_ln_g, r_mu, r_w0, r_w2, r_a0,
           r_a2, r_k_k, r_k_a, r_r_k, r_ln_g, r_ln_b, w_out, final_g):
    depth = w_in.shape[0]
    assert depth == 1, "the final RMSNorm is fused into the single layer's output projection"
    bp = x_prompt.shape[0]
    ctx_state0 = (jnp.zeros((bp, N_DIR, M_HEADS, M_DK, M_DK), F32),
                  jnp.zeros((bp, N_DIR, M_HEADS, M_DK), F32),
                  jnp.full((bp, N_DIR, M_HEADS), -jnp.inf, F32),
                  jnp.zeros((bp, N_DIR, R_HEADS, R_N, R_N), F32))
    l = 0
    w_l = w_in[l]
    gate_lo = 5 * M_WIDTH
    gate_hi = gate_lo + M_GATE_COLS
    w_mproj = jnp.concatenate(
        [w_l[:, :gate_hi], jnp.zeros((D_MODEL, GATE_PAD - M_GATE_COLS), F32)], axis=1).astype(BF16)
    w_rproj = w_l[:, gate_hi:].astype(BF16)
    zpad = jnp.zeros((LORA, R_WIDTH), F32)
    w_lora = jnp.stack([jnp.concatenate([r_w2[l, 0], zpad]), jnp.concatenate([zpad, r_w2[l, 1]]),
                        jnp.concatenate([r_a2[l, 0], zpad]), jnp.concatenate([zpad, r_a2[l, 1]])])
    lw = (w_mproj, w_rproj, m_conv_w[l], m_conv_b[l], m_gate_b[l], m_ln_g[l], r_mu[l], r_w0[l], w_lora,
          r_a0[l], r_k_k[l], r_k_a[l], r_r_k[l], r_ln_g[l], r_ln_b[l], w_out[l].astype(BF16))
    n_cond = 1 + c.shape[0]
    cond = jnp.concatenate([c_ctx[None, :], c, jnp.zeros((16 - n_cond, D_MODEL), F32)], axis=0)
    mod = _mm(jax.nn.silu(cond), w_ada[l], 16, 1536)[:n_cond] + b_ada[l]
    mod_p = mod[0:1, None, :]
    mod_s = mod[1:, None, :]
    y_prompt, (Cp, np_, mp, Sp) = _block(x_prompt, mod_p, ctx_state0, lw, norm_g[l], final_g, False)
    st_s = (state_mlstm_C[:, l], state_mlstm_n[:, l], state_mlstm_m[:, l], state_rwkv_S[:, l])
    y_sample, _ = _block(x_sample, mod_s, st_s, lw, norm_g[l], final_g, True)
    return (y_prompt, y_sample, Cp[:, None], np_[:, None], mp[:, None], Sp[:, None])
```

```python
import functools

import jax
import jax.numpy as jnp
import numpy as np
from jax import lax
from jax.experimental import pallas as pl
from jax.experimental.pallas import tpu as pltpu

D_MODEL = 2048
GRID_W = 64
N_DIR = 2
M_WIDTH = D_MODEL // 2
M_HEADS = 4
M_DK = M_WIDTH // M_HEADS
R_WIDTH = D_MODEL - M_WIDTH
R_N = 64
R_HEADS = R_WIDTH // R_N
LORA = 64
CONV_K = 3
CHUNK = 64
EPS = 1e-6
LNX_EPS = 64e-5
M_GATE_COLS = N_DIR * 2 * M_HEADS
SHIFT_COLS = 3 * R_WIDTH + 2 * N_DIR * LORA

LANES = 128
SUBLANES = 8
HEAD_PAIRS = R_WIDTH // LANES
GATE_PAD = 2 * LANES
M_PROJ_COLS = 5 * M_WIDTH + GATE_PAD
R_PROJ_COLS = R_WIDTH + SHIFT_COLS
RZ_BLK0 = 0
RS_BLK0 = R_WIDTH // LANES
VMEM_LIMIT = 48 * 1024 * 1024

F32 = jnp.float32
BF16 = jnp.bfloat16


def _params(n_axes, vmem_limit=VMEM_LIMIT):
    return pltpu.CompilerParams(dimension_semantics=("arbitrary",) * n_axes,
                                vmem_limit_bytes=vmem_limit)


def _sigmoid(x):
    return 1.0 / (1.0 + jnp.exp(-x))


def _silu(x):
    return x * _sigmoid(x)


def _softplus(x):
    return jnp.maximum(x, 0.0) + jnp.log(1.0 + jnp.exp(-jnp.abs(x)))


def _mm_kernel(x_ref, w_ref, o_ref):
    o_ref[...] = jnp.dot(x_ref[...].astype(BF16), w_ref[...].astype(BF16),
                         preferred_element_type=F32)


def _mm(x, w, tm, tn):
    M, K = x.shape
    N = w.shape[1]
    assert M % tm == 0 and N % tn == 0
    return pl.pallas_call(
        _mm_kernel,
        grid=(N // tn, M // tm),
        in_specs=[pl.BlockSpec((tm, K), lambda j, i: (i, 0)),
                  pl.BlockSpec((K, tn), lambda j, i: (0, j))],
        out_specs=pl.BlockSpec((tm, tn), lambda j, i: (i, j)),
        out_shape=jax.ShapeDtypeStruct((M, N), F32),
        compiler_params=_params(2),
        name="proj_mm",
    )(x, w)


PROJ_ROWS = 512


def _proj_kernel(x_ref, sc_ref, sh_ref, g_ref, w_ref, o_ref, h_s, *, token_major):
    @pl.when(pl.program_id(1) == 0)
    def _():
        def modulated(x):
            y = x * lax.rsqrt(jnp.mean(x * x, axis=-1, keepdims=True) + EPS) * g_ref[...]
            return y * (1.0 + sc_ref[...]) + sh_ref[...]

        if token_major:
            n_b, tt, _ = x_ref.shape

            def body(t, carry):
                r0 = pl.multiple_of(t * n_b, SUBLANES)
                h_s[pl.ds(r0, n_b), :] = modulated(x_ref[:, t, :])
                return carry

            lax.fori_loop(0, tt, body, 0, unroll=4)
        else:
            h_s[...] = modulated(x_ref[...])

    o_ref[...] = jnp.dot(h_s[...].astype(BF16), w_ref[...], preferred_element_type=F32)


def _proj(x, scale, shift, norm_g, w, tn, token_major):
    B, T, D = x.shape
    N = w.shape[1]
    per_b = scale.shape[0] == B
    if token_major:
        tt = min(PROJ_ROWS // B, T)
        rows = tt * B
        grid = (T // tt, N // tn)
        x_spec = pl.BlockSpec((B, tt, D), lambda i, j: (0, i, 0))
        mod_spec = pl.BlockSpec((scale.shape[0], D), lambda i, j: (0, 0))
        scale, shift = scale[:, 0], shift[:, 0]
    else:
        rows = min(PROJ_ROWS, T)
        per_seq = T // rows
        grid = (B * per_seq, N // tn)
        x_spec = pl.BlockSpec((None, rows, D), lambda i, j: (i // per_seq, i % per_seq, 0))
        mod_spec = pl.BlockSpec((None, 1, D), lambda i, j: (i // per_seq if per_b else 0, 0, 0))
    return pl.pallas_call(
        functools.partial(_proj_kernel, token_major=token_major),
        grid=grid,
        in_specs=[x_spec, mod_spec, mod_spec, pl.BlockSpec((1, D), lambda i, j: (0, 0)),
                  pl.BlockSpec((D, tn), lambda i, j: (0, j))],
        out_specs=pl.BlockSpec((rows, tn), lambda i, j: (i, j)),
        out_shape=jax.ShapeDtypeStruct((B * T, N), F32),
        scratch_shapes=[pltpu.VMEM((rows, D), F32)],
        compiler_params=_params(2),
        name="proj_tm" if token_major else "proj_bm",
    )(x, scale, shift, norm_g, w)


MLSTM_UNROLL = 4
MLSTM_VMEM_LIMIT = 56 * 1024 * 1024


def _mlstm_kernel(q_ref, k_ref, v_ref, o_ref, z_ref, wq_ref, wk_ref, bq_ref, bk_ref, lng_ref,
                  gi_ref, gf_ref, c0_ref, n0_ref, m0_ref,
                  hm_ref, c_ref, n_ref, m_ref, qs, ks, hs_f, hs_b, bcol_s, icol_s, rmax_s, rows_s):
    n_tok = q_ref.shape[0]
    nc = n_tok // CHUNK
    L = CHUNK
    tpos = lax.broadcasted_iota(jnp.int32, (n_tok, M_DK), 0)

    def conv_silu(x_ref, w_ref, b_ref):
        x = x_ref[...]
        prev = jnp.where(tpos == 0, 0.0, pltpu.roll(x, 1, 0))
        nxt = jnp.where(tpos == n_tok - 1, 0.0, pltpu.roll(x, n_tok - 1, 0))
        return _silu(prev * w_ref[0:1, :] + x * w_ref[1:2, :] + nxt * w_ref[2:3, :] + b_ref[...])

    qs[...] = conv_silu(q_ref, wq_ref, bq_ref)
    ks[...] = conv_silu(k_ref, wk_ref, bk_ref) * (M_DK ** -0.5)
    c_ref[...] = c0_ref[...]

    row = lax.broadcasted_iota(jnp.int32, (L, L), 0)
    col = lax.broadcasted_iota(jnp.int32, (L, L), 1)
    eye = row == col
    neg_inf = jnp.float32(-jnp.inf)

    def masks(dirn):
        seen = col <= row if dirn == 0 else col >= row
        seen_t = row <= col if dirn == 0 else row >= col
        return seen, seen_t

    def gate_stats(dirn, cc):
        seen, seen_t = masks(dirn)
        idx = dirn * nc + cc
        f_row = gf_ref[dirn, pl.ds(cc, 1), :]
        i_row = gi_ref[dirn, pl.ds(cc, 1), :]
        f_col = jnp.sum(jnp.where(eye, f_row, 0.0), axis=1, keepdims=True)
        i_col = jnp.sum(jnp.where(eye, i_row, 0.0), axis=1, keepdims=True)
        b_col = jnp.sum(jnp.where(seen, f_row, 0.0), axis=1, keepdims=True)
        b_row = jnp.sum(jnp.where(seen_t, f_col, 0.0), axis=0, keepdims=True)
        dmat = jnp.where(seen, b_col - b_row + i_row, neg_inf)
        b_last = jnp.sum(f_row, axis=1, keepdims=True)
        g_row = b_last - b_row + i_row
        bcol_s[idx] = jnp.broadcast_to(b_col, (L, LANES))
        icol_s[idx] = jnp.broadcast_to(i_col, (L, LANES))
        rmax_s[idx] = jnp.broadcast_to(jnp.max(dmat, axis=1, keepdims=True), (L, LANES))
        rows_s[idx, 0:1, 0:L] = b_row
        rows_s[idx, 1:2, 0:L] = i_row
        rows_s[idx, 2:3, :] = jnp.broadcast_to(b_last, (1, LANES))
        rows_s[idx, 3:4, :] = jnp.broadcast_to(jnp.max(g_row, axis=1, keepdims=True), (1, LANES))

    def stats_body(cc, carry):
        gate_stats(0, cc)
        gate_stats(1, cc)
        return carry

    lax.fori_loop(0, nc, stats_body, 0, unroll=min(4, nc))

    def wide(x):
        return jnp.concatenate([x] * (M_DK // LANES), axis=1)

    def chunk(dirn, cc, n, m):
        seen, _ = masks(dirn)
        idx = dirn * nc + cc
        r0 = pl.multiple_of(cc * L, L)
        q = qs[pl.ds(r0, L), :]
        k = ks[pl.ds(r0, L), :]
        v = v_ref[pl.ds(r0, L), :]
        b_col = bcol_s[idx]
        b_row = rows_s[idx, 0:1, 0:L]
        i_row = rows_s[idx, 1:2, 0:L]
        b_last = rows_s[idx, 2:3, :]
        inter = b_col + m
        mt = jnp.maximum(inter, rmax_s[idx])
        dmat = jnp.where(seen, b_col[:, :L] - b_row + i_row, neg_inf)
        qb = q.astype(BF16)
        kb = k.astype(BF16)
        vb = v.astype(BF16)
        qk = lax.dot_general(qb, kb, (((1,), (1,)), ((), ())), preferred_element_type=F32)
        a = jnp.exp(dmat - mt[:, :L]) * qk
        s_in = jnp.exp(inter - mt)
        c = c_ref[dirn]
        qc = jnp.dot(qb, c.astype(BF16), preferred_element_type=F32)
        av = jnp.dot(a.astype(BF16), vb, preferred_element_type=F32)
        num = wide(s_in) * qc + av
        den = s_in * jnp.sum(q * n, axis=1, keepdims=True) + jnp.sum(a, axis=1, keepdims=True)
        hs = hs_f if dirn == 0 else hs_b
        hs[pl.ds(r0, L), :] = num / wide(jnp.maximum(jnp.abs(den), jnp.exp(-mt)))
        m_new = jnp.maximum(b_last + m, rows_s[idx, 3:4, :])
        decay = wide(jnp.exp(b_last + m - m_new))
        g_col = b_last - b_col + icol_s[idx]
        wk = wide(jnp.exp(g_col - m_new)) * k
        c_ref[dirn] = decay * c + lax.dot_general(
            wk.astype(BF16), vb, (((0,), (0,)), ((), ())), preferred_element_type=F32)
        n_new = decay * n + jnp.sum(wk, axis=0, keepdims=True)
        return n_new, m_new

    def body(ci, carry):
        n_f, m_f, n_b, m_b = carry
        n_f, m_f = chunk(0, ci, n_f, m_f)
        n_b, m_b = chunk(1, nc - 1 - ci, n_b, m_b)
        return n_f, m_f, n_b, m_b

    n_f, m_f, n_b, m_b = lax.fori_loop(
        0, nc, body, (n0_ref[0], m0_ref[0], n0_ref[1], m0_ref[1]), unroll=MLSTM_UNROLL)
    n_ref[0] = n_f
    n_ref[1] = n_b
    m_ref[0] = m_f
    m_ref[1] = m_b

    hm = _sigmoid(o_ref[...]) * (hs_f[...] + hs_b[...])
    hc = hm - jnp.mean(hm, axis=1, keepdims=True)
    hn = hc * lax.rsqrt(jnp.mean(hc * hc, axis=1, keepdims=True) + EPS)
    hm_ref[...] = (hn * lng_ref[...] * _silu(z_ref[...])).astype(BF16)


def _mlstm(p, conv_w, conv_b, ln_g, gi, gf, c0, n0, m0):
    B, T, _ = p.shape
    nc = T // CHUNK
    tiles_per_part = M_WIDTH // M_DK

    def col_spec(part):
        return pl.BlockSpec((None, T, M_DK), lambda b, h: (b, 0, part * tiles_per_part + h))

    def par_spec(rows, part):
        return pl.BlockSpec((rows, M_DK), lambda b, h: (0, part * tiles_per_part + h))

    gate_spec = pl.BlockSpec((N_DIR, None, None, nc, CHUNK), lambda b, h: (0, b, h, 0, 0))
    c_spec = pl.BlockSpec((None, N_DIR, None, M_DK, M_DK), lambda b, h: (b, 0, h, 0, 0))
    n_spec = pl.BlockSpec((None, N_DIR, None, 1, M_DK), lambda b, h: (b, 0, h, 0, 0))
    m_spec = pl.BlockSpec((None, N_DIR, None, 1, LANES), lambda b, h: (b, 0, h, 0, 0))
    seq = pltpu.VMEM((T, M_DK), F32)
    stat = pltpu.VMEM((N_DIR * nc, CHUNK, LANES), F32)
    return pl.pallas_call(
        _mlstm_kernel,
        grid=(B, M_HEADS),
        in_specs=[col_spec(0), col_spec(1), col_spec(2), col_spec(3), col_spec(4),
                  par_spec(CONV_K, 0), par_spec(CONV_K, 1), par_spec(1, 0), par_spec(1, 1),
                  par_spec(1, 0), gate_spec, gate_spec, c_spec, n_spec, m_spec],
        out_specs=[pl.BlockSpec((None, T, M_DK), lambda b, h: (b, 0, h)), c_spec, n_spec, m_spec],
        out_shape=[jax.ShapeDtypeStruct((B, T, M_WIDTH), BF16),
                   jax.ShapeDtypeStruct((B, N_DIR, M_HEADS, M_DK, M_DK), F32),
                   jax.ShapeDtypeStruct((B, N_DIR, M_HEADS, 1, M_DK), F32),
                   jax.ShapeDtypeStruct((B, N_DIR, M_HEADS, 1, LANES), F32)],
        scratch_shapes=[seq, seq, seq, seq, stat, stat, stat,
                        pltpu.VMEM((N_DIR * nc, SUBLANES, LANES), F32)],
        compiler_params=_params(2, MLSTM_VMEM_LIMIT),
        name="mlstm_scan",
    )(p, p, p, p, p, conv_w, conv_w, conv_b, conv_b, ln_g, gi, gf, c0, n0, m0)


MIX_SLOTS = 3
LORA_SLOTS = 2 * N_DIR
MIX_TT = 512
LORA_TT = 128
MIX_TILE0 = SHIFT_COLS // LANES - 2


def _prep_kernel(x_ref, lo_ref, hi_ref, mu_ref, *rest, grid_shift, n_tok, lora):
    tblk = pl.program_id(0)
    c = pl.program_id(1)
    x = x_ref[...]
    tt, n_b, _ = x.shape
    t = tblk * tt + lax.broadcasted_iota(jnp.int32, x.shape, 0)
    lane4 = lax.broadcasted_iota(jnp.int32, x.shape, 2) % 4
    xe = jnp.concatenate([lo_ref[...], x, hi_ref[...]], axis=0)
    prev = xe[GRID_W - 1:GRID_W - 1 + tt]
    nxt = xe[GRID_W + 1:GRID_W + 1 + tt]
    if grid_shift:
        col = t % GRID_W
        left = jnp.where(col == 0, 0.0, prev)
        right = jnp.where(col == GRID_W - 1, 0.0, nxt)
        up = jnp.where(t < GRID_W, 0.0, xe[:tt])
        down = jnp.where(t >= n_tok - GRID_W, 0.0, xe[2 * GRID_W:])
        sh = jnp.where(lane4 == 0, left, jnp.where(lane4 == 1, right, jnp.where(lane4 == 2, up, down)))
    else:
        sh = jnp.where(lane4 % 2 == 0, jnp.where(t == 0, 0.0, prev), jnp.where(t == n_tok - 1, 0.0, nxt))
    mixed = x + mu_ref[...] * (sh - x)

    def emit(val, dst):
        v4 = val.reshape(tt // 2, 2, n_b, LANES)
        even, odd = v4[:, 0], v4[:, 1]
        if n_b == SUBLANES:
            lo = lax.broadcasted_iota(jnp.int32, even.shape, 2) < R_N
            dst(0, jnp.where(lo, even, pltpu.roll(odd, R_N, 2)))
            dst(1, jnp.where(lo, pltpu.roll(even, R_N, 2), odd))
        else:
            dst(0, even)
            dst(1, odd)

    if not lora:
        (o_ref,) = rest

        def dst(row, val):
            o_ref[:, row] = val

        emit(mixed, dst)
    else:
        w_ref, o_ref = rest
        act = jnp.where(c < N_DIR, jnp.tanh(mixed), mixed)
        prod = jnp.dot(act.reshape(tt * n_b, LANES).astype(BF16), w_ref[...].astype(BF16),
                       preferred_element_type=F32)
        for hp in range(HEAD_PAIRS):
            def dst(row, val, hp=hp):
                o_ref[:, row, hp] = val

            emit(prod[:, hp * LANES:(hp + 1) * LANES].reshape(tt, n_b, LANES), dst)


def _rwkv_prep(p, r_mu, w_lora, grid_shift):
    T, B, _ = p.shape
    n_halo = T // GRID_W

    def specs(tt, col):
        per_blk = tt // GRID_W
        return [pl.BlockSpec((tt, B, LANES), lambda t, c: (t, 0, RS_BLK0 + col(c))),
                pl.BlockSpec((GRID_W, B, LANES),
                             lambda t, c: (jnp.maximum(t * per_blk - 1, 0), 0, RS_BLK0 + col(c))),
                pl.BlockSpec((GRID_W, B, LANES),
                             lambda t, c: (jnp.minimum((t + 1) * per_blk, n_halo - 1), 0, RS_BLK0 + col(c))),
                pl.BlockSpec((1, LANES), lambda t, c: (0, col(c)))]

    tt = min(MIX_TT, T)
    mix = pl.pallas_call(
        functools.partial(_prep_kernel, grid_shift=grid_shift, n_tok=T, lora=False),
        grid=(T // tt, MIX_SLOTS * HEAD_PAIRS),
        in_specs=specs(tt, lambda c: c),
        out_specs=pl.BlockSpec((None, tt // 2, 2, None, B, LANES),
                               lambda t, c: (c // HEAD_PAIRS, t, 0, c % HEAD_PAIRS, 0, 0)),
        out_shape=jax.ShapeDtypeStruct((MIX_SLOTS, T // 2, 2, HEAD_PAIRS, B, LANES), F32),
        compiler_params=_params(2),
        name="rwkv_prep",
    )(p, p, p, r_mu)
    tt = min(LORA_TT, T)
    lora = pl.pallas_call(
        functools.partial(_prep_kernel, grid_shift=grid_shift, n_tok=T, lora=True),
        grid=(T // tt, LORA_SLOTS),
        in_specs=specs(tt, lambda c: MIX_TILE0 + c // N_DIR)
        + [pl.BlockSpec((None, LANES, R_WIDTH), lambda t, c: (c, 0, 0))],
        out_specs=pl.BlockSpec((None, tt // 2, 2, HEAD_PAIRS, B, LANES),
                               lambda t, c: (c, t, 0, 0, 0, 0)),
        out_shape=jax.ShapeDtypeStruct((LORA_SLOTS, T // 2, 2, HEAD_PAIRS, B, LANES), F32),
        compiler_params=_params(2),
        name="rwkv_lora",
    )(p, p, p, r_mu, w_lora)
    return mix, lora


RWKV_TB = 32
J_UNROLL = 16
PREP_UNROLL = 4


def _chain_tiles(ref, pair, group, n_b):
    if n_b == SUBLANES:
        xt = ref[pair].reshape(LANES, LANES).T
        return [xt[:R_N], xt[R_N:]]
    tiles = []
    for parity in range(2):
        xt = ref[pair, parity].reshape(LANES, LANES).T
        tiles.append(jnp.where(group == 0, xt[:R_N], xt[R_N:]))
    return tiles


def _rwkv_kernel(r_ref, k_ref, v_ref, lw_ref, la_ref, pd_ref, ps_ref, s0_ref,
                 y_ref, c_ref, s_ref, kh_s, rh_s, vv_s, kt_s, bt_s, g_s, *, n_b):
    dirn = pl.program_id(0)
    group = pl.program_id(1)
    tb = pl.program_id(2)
    n_tok = y_ref.shape[0]
    n_pairs = n_tok // 2
    fwd = dirn == 0

    @pl.when(tb == 0)
    def _():
        s_ref[...] = s0_ref[...]

    w0, a0 = pd_ref[0], pd_ref[1]
    k_k, k_a, r_k = ps_ref[0], ps_ref[1], ps_ref[2]

    def prep(q, g):
        pair = q + dirn * (n_pairs - 1 - 2 * q)
        tiles = [_chain_tiles(ref, pair, group, n_b) for ref in (r_ref, k_ref, v_ref, lw_ref, la_ref)]
        for i in range(2):
            r, kraw, v, lw, la = (jnp.where(fwd, x[i], x[1 - i]) for x in tiles)
            pos = 2 * q + i
            tok = 2 * pair + i + dirn * (1 - 2 * i)
            wlog = -_softplus(-(w0 + lw)) - 0.5
            a = _sigmoid(a0 + la)
            kk = kraw * k_k
            kk = kk / jnp.maximum(jnp.sqrt(jnp.sum(kk * kk, axis=0, keepdims=True)), 1e-12)
            kz = kraw * (1.0 + (a - 1.0) * k_a)
            c_ref[tok] = jnp.sum(r * kz * r_k, axis=0, keepdims=True) * v
            kh_s[pos] = kk * g
            g = g * jnp.exp(-jnp.exp(wlog))
            g_inv = 1.0 / g
            vv_s[pos] = v
            bt_s[pos] = kk * a * g_inv
            kt_s[pos] = kz * g_inv
            rh_s[pos] = r * g
        return g

    g_s[...] = lax.fori_loop(0, n_pairs, prep, jnp.ones((R_N, LANES), F32), unroll=PREP_UNROLL)

    zeros = jnp.zeros((R_N, LANES), F32)

    def dot_first(jb, u):
        for jj in range(J_UNROLL):
            j = jb * J_UNROLL + jj
            u = u + s_ref[j] * kh_s[0, pl.ds(j, 1), :]
        return u

    def step(pos, u):
        tok = pos + dirn * (n_tok - 1 - 2 * pos)
        nxt = jnp.minimum(pos + 1, n_tok - 1)
        sa = -u
        vt = vv_s[pos]

        def update(jb, carry):
            y, u_next = carry
            for jj in range(J_UNROLL):
                j = jb * J_UNROLL + jj
                sn = s_ref[j] + sa * bt_s[pos, pl.ds(j, 1), :] + vt * kt_s[pos, pl.ds(j, 1), :]
                s_ref[j] = sn
                y = y + sn * rh_s[pos, pl.ds(j, 1), :]
                u_next = u_next + sn * kh_s[nxt, pl.ds(j, 1), :]
            return y, u_next

        y, u_next = lax.fori_loop(0, R_N // J_UNROLL, update, (zeros, zeros))
        y_ref[tok] = y
        return u_next

    lax.fori_loop(0, n_tok, step, lax.fori_loop(0, R_N // J_UNROLL, dot_first, zeros))

    def rescale(jb, carry):
        for jj in range(J_UNROLL):
            j = jb * J_UNROLL + jj
            s_ref[j] = s_ref[j] * g_s[pl.ds(j, 1), :]
        return carry

    lax.fori_loop(0, R_N // J_UNROLL, rescale, 0)


def _rwkv(xs, xl, p_dir, p_sh, s0):
    _, half_t, _, _, B, _ = xs.shape
    T = 2 * half_t
    G = p_sh.shape[0]
    tb = min(RWKV_TB, T)
    nt = T // tb
    rev = lambda dd, t: t + dd * (nt - 1 - 2 * t)

    def slot(s):
        return pl.BlockSpec((None, tb // 2, 2, HEAD_PAIRS, B, LANES),
                            lambda dd, g, t: (s(dd), rev(dd, t), 0, 0, 0, 0))

    state = pl.BlockSpec((None, None, R_N, R_N, LANES), lambda dd, g, t: (dd, g, 0, 0, 0))
    out = pl.BlockSpec((None, tb, R_N, LANES), lambda dd, g, t: (dd, rev(dd, t), 0, g))
    scratch = pltpu.VMEM((tb, R_N, LANES), F32)
    return pl.pallas_call(
        functools.partial(_rwkv_kernel, n_b=B),
        grid=(N_DIR, G, nt),
        in_specs=[slot(lambda dd: 0), slot(lambda dd: 1), slot(lambda dd: 2),
                  slot(lambda dd: dd), slot(lambda dd: N_DIR + dd),
                  pl.BlockSpec((None, None, 2, R_N, LANES), lambda dd, g, t: (dd, g, 0, 0, 0)),
                  pl.BlockSpec((None, 3, R_N, LANES), lambda dd, g, t: (g, 0, 0, 0)),
                  state],
        out_specs=[out, out, state],
        out_shape=[jax.ShapeDtypeStruct((N_DIR, T, R_N, G * LANES), F32),
                   jax.ShapeDtypeStruct((N_DIR, T, R_N, G * LANES), F32),
                   jax.ShapeDtypeStruct((N_DIR, G, R_N, R_N, LANES), F32)],
        scratch_shapes=[scratch] * 5 + [pltpu.VMEM((R_N, LANES), F32)],
        compiler_params=_params(3),
        name="rwkv_scan",
    )(xs, xs, xs, xl, xl, p_dir, p_sh, s0)


def _post_kernel(y_ref, c_ref, z_ref, ln_ref, o_ref, nat_s, *, n_b):
    n_tok = y_ref.shape[1]
    lo = lax.broadcasted_iota(jnp.int32, (R_N, LANES), 1) < R_N

    def normed(t, g):
        lanes = pl.ds(g * LANES, LANES)
        y = y_ref[0, t, :, lanes] + y_ref[1, t, :, lanes]
        mean = jnp.mean(y, axis=0, keepdims=True)
        yc = y - mean
        var = jnp.mean(yc * yc, axis=0, keepdims=True)
        return (yc * lax.rsqrt(var + LNX_EPS) * ln_ref[0, :, lanes] + ln_ref[1, :, lanes]
                + c_ref[0, t, :, lanes] + c_ref[1, t, :, lanes])

    rows_per_tok = HEAD_PAIRS * n_b

    def gate(t, x):
        z = z_ref[t]
        for hp in range(HEAD_PAIRS):
            r0 = pl.multiple_of(t * rows_per_tok + hp * n_b, SUBLANES)
            nat_s[pl.ds(r0, n_b), :] = x[hp * n_b:(hp + 1) * n_b] * _silu(z[:, hp * LANES:(hp + 1) * LANES])

    def body(tp, carry):
        t0 = tp * 2
        if n_b == SUBLANES:
            a, b = normed(t0, 0), normed(t0 + 1, 0)
            xt = jnp.concatenate([jnp.where(lo, a, pltpu.roll(b, R_N, 1)),
                                  jnp.where(lo, pltpu.roll(a, R_N, 1), b)], axis=0)
            x = xt.T
            for i in range(2):
                gate(t0 + i, x[i * R_N:(i + 1) * R_N])
        else:
            for i in range(2):
                gate(t0 + i, jnp.concatenate([normed(t0 + i, 0), normed(t0 + i, 1)], axis=0).T)
        return carry

    lax.fori_loop(0, n_tok // 2, body, 0, unroll=2)

    for hp in range(HEAD_PAIRS):
        for b in range(n_b):
            o_ref[b, :, hp * LANES:(hp + 1) * LANES] = nat_s[pl.ds(hp * n_b + b, n_tok, stride=rows_per_tok), :]


def _rwkv_post(y, c, p, ln):
    _, T, _, CH = y.shape
    B = p.shape[1]
    tb = min(RWKV_TB, T)
    chain = pl.BlockSpec((N_DIR, tb, R_N, CH), lambda t: (0, t, 0, 0))
    nat = pl.BlockSpec((tb, B, R_WIDTH), lambda t: (t, 0, RZ_BLK0))
    return pl.pallas_call(
        functools.partial(_post_kernel, n_b=B),
        grid=(T // tb,),
        in_specs=[chain, chain, nat, pl.BlockSpec((2, R_N, CH), lambda t: (0, 0, 0))],
        out_specs=pl.BlockSpec((B, tb, R_WIDTH), lambda t: (0, t, 0)),
        out_shape=jax.ShapeDtypeStruct((B, T, R_WIDTH), F32),
        scratch_shapes=[pltpu.VMEM((tb * HEAD_PAIRS * B, LANES), F32)],
        compiler_params=_params(1),
        name="rwkv_post",
    )(y, c, p, ln)


OUT_TM = 512


def _out_kernel(hm_ref, yr_ref, w_ref, x_ref, gate_ref, g_ref, o_ref):
    cat = jnp.concatenate([hm_ref[...], yr_ref[...].astype(BF16)], axis=1)
    out = jnp.dot(cat, w_ref[...], preferred_element_type=F32)
    z = x_ref[...] + gate_ref[...] * out
    o_ref[...] = z * lax.rsqrt(jnp.mean(z * z, axis=-1, keepdims=True) + EPS) * g_ref[...]


def _out_proj(hm, yr, w_out, x, gate, final_g):
    B, T, D = x.shape
    tm = min(OUT_TM, T)
    per_b = gate.shape[0] == B
    row = lambda b, t: (b, t, 0)
    return pl.pallas_call(
        _out_kernel,
        grid=(B, T // tm),
        in_specs=[pl.BlockSpec((None, tm, M_WIDTH), row),
                  pl.BlockSpec((None, tm, R_WIDTH), row),
                  pl.BlockSpec((D, D), lambda b, t: (0, 0)),
                  pl.BlockSpec((None, tm, D), row),
                  pl.BlockSpec((None, 1, D), lambda b, t: (b if per_b else 0, 0, 0)),
                  pl.BlockSpec((1, D), lambda b, t: (0, 0))],
        out_specs=pl.BlockSpec((None, tm, D), row),
        out_shape=jax.ShapeDtypeStruct((B, T, D), F32),
        compiler_params=_params(2),
        name="out_proj",
    )(hm, yr, w_out, x, gate, final_g)


def _chain_param(p, n_b):
    q = p.reshape(HEAD_PAIRS, 2, R_N)
    if n_b == SUBLANES:
        t = jnp.broadcast_to(q.transpose(2, 1, 0)[..., None], (R_N, 2, HEAD_PAIRS, n_b))
        return t.reshape(1, R_N, LANES)
    t = jnp.broadcast_to(q.transpose(1, 2, 0)[..., None], (2, R_N, HEAD_PAIRS, n_b))
    return t.reshape(2, R_N, LANES)


def _chain_state(s, n_b):
    s6 = s.reshape(n_b, N_DIR, HEAD_PAIRS, 2, R_N, R_N)
    if n_b == SUBLANES:
        return s6.transpose(1, 5, 4, 3, 2, 0).reshape(N_DIR, 1, R_N, R_N, LANES)
    return s6.transpose(1, 3, 5, 4, 2, 0).reshape(N_DIR, 2, R_N, R_N, LANES)


def _unchain_state(s, n_b):
    if n_b == SUBLANES:
        s6 = s.reshape(N_DIR, R_N, R_N, 2, HEAD_PAIRS, n_b).transpose(5, 0, 4, 3, 2, 1)
    else:
        s6 = s.reshape(N_DIR, 2, R_N, R_N, HEAD_PAIRS, n_b).transpose(5, 0, 4, 1, 3, 2)
    return s6.reshape(n_b, N_DIR, R_HEADS, R_N, R_N)


def _block(x, mod, st, lw, norm_g, final_g, grid):
    (w_mproj, w_rproj, m_conv_w, m_conv_b, m_gate_b, m_ln_g, r_mu, r_w0, w_lora, r_a0,
     r_k_k, r_k_a, r_r_k, r_ln_g, r_ln_b, w_out) = lw
    C0, n0, m0, S0 = st
    B, T, D = x.shape
    assert B in (SUBLANES, 2 * SUBLANES), "chain layout is written for 8 or 16 sequences"
    shift, scale, gate = jnp.split(mod, 3, axis=-1)
    g2 = norm_g.reshape(1, D)
    p = _proj(x, scale, shift, g2, w_mproj, M_PROJ_COLS // 3, False).reshape(B, T, M_PROJ_COLS)
    pr = _proj(x, scale, shift, g2, w_rproj, R_PROJ_COLS // 2, True)
    pr = pr.reshape(T, B, R_PROJ_COLS)
    mg = p[..., 5 * M_WIDTH:5 * M_WIDTH + M_GATE_COLS]

    gates = (mg.reshape(B, T, N_DIR, 2, M_HEADS) + m_gate_b).transpose(2, 3, 0, 4, 1)
    nc = T // CHUNK
    log_i = gates[:, 0].reshape(N_DIR, B, M_HEADS, nc, CHUNK)
    log_f = jax.nn.log_sigmoid(gates[:, 1]).reshape(N_DIR, B, M_HEADS, nc, CHUNK)
    hm, Cn, nn, mn = _mlstm(
        p, m_conv_w, m_conv_b.reshape(1, 2 * M_WIDTH), m_ln_g.reshape(1, M_WIDTH), log_i, log_f,
        C0, n0[..., None, :], jnp.broadcast_to(m0[..., None, None], m0.shape + (1, LANES)))

    xs, xl = _rwkv_prep(pr, r_mu.reshape(1, SHIFT_COLS), w_lora, grid)
    p_dir = jnp.stack([jnp.stack([_chain_param(r_w0[z], B), _chain_param(r_a0[z], B)], axis=1)
                       for z in range(N_DIR)])
    p_sh = jnp.stack([_chain_param(r_k_k, B), _chain_param(r_k_a, B),
                      _chain_param(r_r_k.reshape(R_WIDTH), B)], axis=1)
    y_dirs, c_dirs, S_new = _rwkv(xs, xl, p_dir, p_sh, _chain_state(S0, B))
    G = p_sh.shape[0]
    ln = jnp.stack([_chain_param(r_ln_g, B), _chain_param(r_ln_b, B)])
    ln = ln.transpose(0, 2, 1, 3).reshape(2, R_N, G * LANES)
    yr = _rwkv_post(y_dirs, c_dirs, pr, ln)

    out = _out_proj(hm, yr, w_out, x, gate, final_g.reshape(1, D))
    return out, (Cn, nn[..., 0, :], mn[..., 0, 0], _unchain_state(S_new, B))


def kernel(x_prompt, x_sample, state_mlstm_C, state_mlstm_n, state_mlstm_m, state_rwkv_S, c, c_ctx,
           norm_g, w_ada, b_ada, w_in, m_conv_w, m_conv_b, m_gate_b, m_ln_g, r_mu, r_w0, r_w2, r_a0,
           r_a2, r_k_k, r_k_a, r_r_k, r_ln_g, r_ln_b, w_out, final_g):
    depth = w_in.shape[0]
    assert depth == 1, "the final RMSNorm is fused into the single layer's output projection"
    bp = x_prompt.shape[0]
    ctx_state0 = (jnp.zeros((bp, N_DIR, M_HEADS, M_DK, M_DK), F32),
                  jnp.zeros((bp, N_DIR, M_HEADS, M_DK), F32),
                  jnp.full((bp, N_DIR, M_HEADS), -jnp.inf, F32),
                  jnp.zeros((bp, N_DIR, R_HEADS, R_N, R_N), F32))
    l = 0
    w_l = w_in[l]
    gate_lo = 5 * M_WIDTH
    gate_hi = gate_lo + M_GATE_COLS
    w_mproj = jnp.concatenate(
        [w_l[:, :gate_hi], jnp.zeros((D_MODEL, GATE_PAD - M_GATE_COLS), F32)], axis=1).astype(BF16)
    w_rproj = w_l[:, gate_hi:].astype(BF16)
    zpad = jnp.zeros((LORA, R_WIDTH), F32)
    w_lora = jnp.stack([jnp.concatenate([r_w2[l, 0], zpad]), jnp.concatenate([zpad, r_w2[l, 1]]),
                        jnp.concatenate([r_a2[l, 0], zpad]), jnp.concatenate([zpad, r_a2[l, 1]])])
    lw = (w_mproj, w_rproj, m_conv_w[l], m_conv_b[l], m_gate_b[l], m_ln_g[l], r_mu[l], r_w0[l], w_lora,
          r_a0[l], r_k_k[l], r_k_a[l], r_r_k[l], r_ln_g[l], r_ln_b[l], w_out[l].astype(BF16))
    n_cond = 1 + c.shape[0]
    cond = jnp.concatenate([c_ctx[None, :], c, jnp.zeros((16 - n_cond, D_MODEL), F32)], axis=0)
    mod = _mm(jax.nn.silu(cond), w_ada[l], 16, 1536)[:n_cond] + b_ada[l]
    mod_p = mod[0:1, None, :]
    mod_s = mod[1:, None, :]
    y_prompt, (Cp, np_, mp, Sp) = _block(x_prompt, mod_p, ctx_state0, lw, norm_g[l], final_g, False)
    st_s = (state_mlstm_C[:, l], state_mlstm_n[:, l], state_mlstm_m[:, l], state_rwkv_S[:, l])
    y_sample, _ = _block(x_sample, mod_s, st_s, lw, norm_g[l], final_g, True)
    return (y_prompt, y_sample, Cp[:, None], np_[:, None], mp[:, None], Sp[:, None])
```

```python
import functools

import jax
import jax.numpy as jnp
import numpy as np
from jax import lax
from jax.experimental import pallas as pl
from jax.experimental.pallas import tpu as pltpu

D_MODEL = 2048
GRID_W = 64
N_DIR = 2
M_WIDTH = D_MODEL // 2
M_HEADS = 4
M_DK = M_WIDTH // M_HEADS
R_WIDTH = D_MODEL - M_WIDTH
R_N = 64
R_HEADS = R_WIDTH // R_N
LORA = 64
CONV_K = 3
CHUNK = 64
EPS = 1e-6
LNX_EPS = 64e-5
M_GATE_COLS = N_DIR * 2 * M_HEADS
SHIFT_COLS = 3 * R_WIDTH + 2 * N_DIR * LORA

LANES = 128
SUBLANES = 8
HEAD_PAIRS = R_WIDTH // LANES
GATE_PAD = 2 * LANES
M_PROJ_COLS = 5 * M_WIDTH + GATE_PAD
R_PROJ_COLS = R_WIDTH + SHIFT_COLS
RZ_BLK0 = 0
RS_BLK0 = R_WIDTH // LANES
VMEM_LIMIT = 48 * 1024 * 1024

F32 = jnp.float32
BF16 = jnp.bfloat16


def _params(n_axes, vmem_limit=VMEM_LIMIT):
    return pltpu.CompilerParams(dimension_semantics=("arbitrary",) * n_axes,
                                vmem_limit_bytes=vmem_limit)


def _sigmoid(x):
    return 1.0 / (1.0 + jnp.exp(-x))


def _silu(x):
    return x * _sigmoid(x)


def _softplus(x):
    return jnp.maximum(x, 0.0) + jnp.log(1.0 + jnp.exp(-jnp.abs(x)))


def _mm_kernel(x_ref, w_ref, o_ref):
    o_ref[...] = jnp.dot(x_ref[...].astype(BF16), w_ref[...].astype(BF16),
                         preferred_element_type=F32)


def _mm(x, w, tm, tn):
    M, K = x.shape
    N = w.shape[1]
    assert M % tm == 0 and N % tn == 0
    return pl.pallas_call(
        _mm_kernel,
        grid=(N // tn, M // tm),
        in_specs=[pl.BlockSpec((tm, K), lambda j, i: (i, 0)),
                  pl.BlockSpec((K, tn), lambda j, i: (0, j))],
        out_specs=pl.BlockSpec((tm, tn), lambda j, i: (i, j)),
        out_shape=jax.ShapeDtypeStruct((M, N), F32),
        compiler_params=_params(2),
        name="proj_mm",
    )(x, w)


PROJ_ROWS = 512


def _proj_kernel(x_ref, sc_ref, sh_ref, g_ref, w_ref, o_ref, *scratch, token_major):
    def modulated(x):
        y = x * lax.rsqrt(jnp.mean(x * x, axis=-1, keepdims=True) + EPS) * g_ref[...]
        return y * (1.0 + sc_ref[...]) + sh_ref[...]

    if token_major:
        (h_s,) = scratch
        n_b, tt, _ = x_ref.shape

        def body(t, carry):
            r0 = pl.multiple_of(t * n_b, SUBLANES)
            h_s[pl.ds(r0, n_b), :] = modulated(x_ref[:, t, :])
            return carry

        lax.fori_loop(0, tt, body, 0, unroll=4)
        h = h_s[...]
    else:
        h = modulated(x_ref[...])
    o_ref[...] = jnp.dot(h.astype(BF16), w_ref[...], preferred_element_type=F32)


def _proj(x, scale, shift, norm_g, w, tn, token_major):
    B, T, D = x.shape
    N = w.shape[1]
    per_b = scale.shape[0] == B
    if token_major:
        tt = min(PROJ_ROWS // B, T)
        rows = tt * B
        n_row_tiles = T // tt
        x_spec = pl.BlockSpec((B, tt, D), lambda j, i: (0, i, 0))
        mod_spec = pl.BlockSpec((scale.shape[0], D), lambda j, i: (0, 0))
        scale, shift = scale[:, 0], shift[:, 0]
        scratch = [pltpu.VMEM((rows, D), F32)]
    else:
        rows = min(PROJ_ROWS, T) if per_b else min(PROJ_ROWS, B * T)
        n_row_tiles = B * T // rows
        x = x.reshape(B * T, D)
        x_spec = pl.BlockSpec((rows, D), lambda j, i: (i, 0))
        mod_spec = pl.BlockSpec((None, 1, D), lambda j, i: (i * rows // T if per_b else 0, 0, 0))
        scratch = []
    return pl.pallas_call(
        functools.partial(_proj_kernel, token_major=token_major),
        grid=(N // tn, n_row_tiles),
        in_specs=[x_spec, mod_spec, mod_spec, pl.BlockSpec((1, D), lambda j, i: (0, 0)),
                  pl.BlockSpec((D, tn), lambda j, i: (0, j))],
        out_specs=pl.BlockSpec((rows, tn), lambda j, i: (i, j)),
        out_shape=jax.ShapeDtypeStruct((B * T, N), F32),
        scratch_shapes=scratch,
        compiler_params=_params(2),
        name="proj_tm" if token_major else "proj_bm",
    )(x, scale, shift, norm_g, w)


MLSTM_UNROLL = 4
MLSTM_VMEM_LIMIT = 56 * 1024 * 1024


def _mlstm_kernel(q_ref, k_ref, v_ref, o_ref, z_ref, wq_ref, wk_ref, bq_ref, bk_ref, lng_ref,
                  gi_ref, gf_ref, c0_ref, n0_ref, m0_ref,
                  hm_ref, c_ref, n_ref, m_ref, qs, ks, hs_f, hs_b, bcol_s, icol_s, rmax_s, rows_s):
    n_tok = q_ref.shape[0]
    nc = n_tok // CHUNK
    L = CHUNK
    tpos = lax.broadcasted_iota(jnp.int32, (n_tok, M_DK), 0)

    def conv_silu(x_ref, w_ref, b_ref):
        x = x_ref[...]
        prev = jnp.where(tpos == 0, 0.0, pltpu.roll(x, 1, 0))
        nxt = jnp.where(tpos == n_tok - 1, 0.0, pltpu.roll(x, n_tok - 1, 0))
        return _silu(prev * w_ref[0:1, :] + x * w_ref[1:2, :] + nxt * w_ref[2:3, :] + b_ref[...])

    qs[...] = conv_silu(q_ref, wq_ref, bq_ref)
    ks[...] = conv_silu(k_ref, wk_ref, bk_ref) * (M_DK ** -0.5)
    c_ref[...] = c0_ref[...]

    row = lax.broadcasted_iota(jnp.int32, (L, L), 0)
    col = lax.broadcasted_iota(jnp.int32, (L, L), 1)
    eye = row == col
    neg_inf = jnp.float32(-jnp.inf)

    def masks(dirn):
        seen = col <= row if dirn == 0 else col >= row
        seen_t = row <= col if dirn == 0 else row >= col
        return seen, seen_t

    def gate_stats(dirn, cc):
        seen, seen_t = masks(dirn)
        idx = dirn * nc + cc
        f_row = gf_ref[dirn, pl.ds(cc, 1), :]
        i_row = gi_ref[dirn, pl.ds(cc, 1), :]
        f_col = jnp.sum(jnp.where(eye, f_row, 0.0), axis=1, keepdims=True)
        i_col = jnp.sum(jnp.where(eye, i_row, 0.0), axis=1, keepdims=True)
        b_col = jnp.sum(jnp.where(seen, f_row, 0.0), axis=1, keepdims=True)
        b_row = jnp.sum(jnp.where(seen_t, f_col, 0.0), axis=0, keepdims=True)
        dmat = jnp.where(seen, b_col - b_row + i_row, neg_inf)
        b_last = jnp.sum(f_row, axis=1, keepdims=True)
        g_row = b_last - b_row + i_row
        bcol_s[idx] = jnp.broadcast_to(b_col, (L, LANES))
        icol_s[idx] = jnp.broadcast_to(i_col, (L, LANES))
        rmax_s[idx] = jnp.broadcast_to(jnp.max(dmat, axis=1, keepdims=True), (L, LANES))
        rows_s[idx, 0:1, 0:L] = b_row
        rows_s[idx, 1:2, 0:L] = i_row
        rows_s[idx, 2:3, :] = jnp.broadcast_to(b_last, (1, LANES))
        rows_s[idx, 3:4, :] = jnp.broadcast_to(jnp.max(g_row, axis=1, keepdims=True), (1, LANES))

    def stats_body(cc, carry):
        gate_stats(0, cc)
        gate_stats(1, cc)
        return carry

    lax.fori_loop(0, nc, stats_body, 0, unroll=min(4, nc))

    def wide(x):
        return jnp.concatenate([x] * (M_DK // LANES), axis=1)

    def chunk(dirn, cc, n, m):
        seen, _ = masks(dirn)
        idx = dirn * nc + cc
        r0 = pl.multiple_of(cc * L, L)
        q = qs[pl.ds(r0, L), :]
        k = ks[pl.ds(r0, L), :]
        v = v_ref[pl.ds(r0, L), :]
        b_col = bcol_s[idx]
        b_row = rows_s[idx, 0:1, 0:L]
        i_row = rows_s[idx, 1:2, 0:L]
        b_last = rows_s[idx, 2:3, :]
        inter = b_col + m
        mt = jnp.maximum(inter, rmax_s[idx])
        dmat = jnp.where(seen, b_col[:, :L] - b_row + i_row, neg_inf)
        qb = q.astype(BF16)
        kb = k.astype(BF16)
        vb = v.astype(BF16)
        qk = lax.dot_general(qb, kb, (((1,), (1,)), ((), ())), preferred_element_type=F32)
        a = jnp.exp(dmat - mt[:, :L]) * qk
        s_in = jnp.exp(inter - mt)
        c = c_ref[dirn]
        qc = jnp.dot(qb, c.astype(BF16), preferred_element_type=F32)
        av = jnp.dot(a.astype(BF16), vb, preferred_element_type=F32)
        num = wide(s_in) * qc + av
        den = s_in * jnp.sum(q * n, axis=1, keepdims=True) + jnp.sum(a, axis=1, keepdims=True)
        hs = hs_f if dirn == 0 else hs_b
        hs[pl.ds(r0, L), :] = num / wide(jnp.maximum(jnp.abs(den), jnp.exp(-mt)))
        m_new = jnp.maximum(b_last + m, rows_s[idx, 3:4, :])
        decay = wide(jnp.exp(b_last + m - m_new))
        g_col = b_last - b_col + icol_s[idx]
        wk = wide(jnp.exp(g_col - m_new)) * k
        c_ref[dirn] = decay * c + lax.dot_general(
            wk.astype(BF16), vb, (((0,), (0,)), ((), ())), preferred_element_type=F32)
        n_new = decay * n + jnp.sum(wk, axis=0, keepdims=True)
        return n_new, m_new

    def body(ci, carry):
        n_f, m_f, n_b, m_b = carry
        n_f, m_f = chunk(0, ci, n_f, m_f)
        n_b, m_b = chunk(1, nc - 1 - ci, n_b, m_b)
        return n_f, m_f, n_b, m_b

    n_f, m_f, n_b, m_b = lax.fori_loop(
        0, nc, body, (n0_ref[0], m0_ref[0], n0_ref[1], m0_ref[1]), unroll=MLSTM_UNROLL)
    n_ref[0] = n_f
    n_ref[1] = n_b
    m_ref[0] = m_f
    m_ref[1] = m_b

    hm = _sigmoid(o_ref[...]) * (hs_f[...] + hs_b[...])
    hc = hm - jnp.mean(hm, axis=1, keepdims=True)
    hn = hc * lax.rsqrt(jnp.mean(hc * hc, axis=1, keepdims=True) + EPS)
    hm_ref[...] = (hn * lng_ref[...] * _silu(z_ref[...])).astype(BF16)


def _mlstm(p, conv_w, conv_b, ln_g, gi, gf, c0, n0, m0):
    B, T, _ = p.shape
    nc = T // CHUNK
    tiles_per_part = M_WIDTH // M_DK

    def col_spec(part):
        return pl.BlockSpec((None, T, M_DK), lambda b, h: (b, 0, part * tiles_per_part + h))

    def par_spec(rows, part):
        return pl.BlockSpec((rows, M_DK), lambda b, h: (0, part * tiles_per_part + h))

    gate_spec = pl.BlockSpec((N_DIR, None, None, nc, CHUNK), lambda b, h: (0, b, h, 0, 0))
    c_spec = pl.BlockSpec((None, N_DIR, None, M_DK, M_DK), lambda b, h: (b, 0, h, 0, 0))
    n_spec = pl.BlockSpec((None, N_DIR, None, 1, M_DK), lambda b, h: (b, 0, h, 0, 0))
    m_spec = pl.BlockSpec((None, N_DIR, None, 1, LANES), lambda b, h: (b, 0, h, 0, 0))
    seq = pltpu.VMEM((T, M_DK), F32)
    stat = pltpu.VMEM((N_DIR * nc, CHUNK, LANES), F32)
    return pl.pallas_call(
        _mlstm_kernel,
        grid=(B, M_HEADS),
        in_specs=[col_spec(0), col_spec(1), col_spec(2), col_spec(3), col_spec(4),
                  par_spec(CONV_K, 0), par_spec(CONV_K, 1), par_spec(1, 0), par_spec(1, 1),
                  par_spec(1, 0), gate_spec, gate_spec, c_spec, n_spec, m_spec],
        out_specs=[pl.BlockSpec((None, T, M_DK), lambda b, h: (b, 0, h)), c_spec, n_spec, m_spec],
        out_shape=[jax.ShapeDtypeStruct((B, T, M_WIDTH), BF16),
                   jax.ShapeDtypeStruct((B, N_DIR, M_HEADS, M_DK, M_DK), F32),
                   jax.ShapeDtypeStruct((B, N_DIR, M_HEADS, 1, M_DK), F32),
                   jax.ShapeDtypeStruct((B, N_DIR, M_HEADS, 1, LANES), F32)],
        scratch_shapes=[seq, seq, seq, seq, stat, stat, stat,
                        pltpu.VMEM((N_DIR * nc, SUBLANES, LANES), F32)],
        compiler_params=_params(2, MLSTM_VMEM_LIMIT),
        name="mlstm_scan",
    )(p, p, p, p, p, conv_w, conv_w, conv_b, conv_b, ln_g, gi, gf, c0, n0, m0)


MIX_SLOTS = 3
LORA_SLOTS = 2 * N_DIR
MIX_TT = 512
LORA_TT = 128
MIX_TILE0 = SHIFT_COLS // LANES - 2


def _prep_kernel(x_ref, lo_ref, hi_ref, mu_ref, *rest, grid_shift, n_tok, lora):
    tblk = pl.program_id(0)
    c = pl.program_id(1)
    x = x_ref[...]
    tt, n_b, _ = x.shape
    t = tblk * tt + lax.broadcasted_iota(jnp.int32, x.shape, 0)
    lane4 = lax.broadcasted_iota(jnp.int32, x.shape, 2) % 4
    xe = jnp.concatenate([lo_ref[...], x, hi_ref[...]], axis=0)
    prev = xe[GRID_W - 1:GRID_W - 1 + tt]
    nxt = xe[GRID_W + 1:GRID_W + 1 + tt]
    if grid_shift:
        col = t % GRID_W
        left = jnp.where(col == 0, 0.0, prev)
        right = jnp.where(col == GRID_W - 1, 0.0, nxt)
        up = jnp.where(t < GRID_W, 0.0, xe[:tt])
        down = jnp.where(t >= n_tok - GRID_W, 0.0, xe[2 * GRID_W:])
        sh = jnp.where(lane4 == 0, left, jnp.where(lane4 == 1, right, jnp.where(lane4 == 2, up, down)))
    else:
        sh = jnp.where(lane4 % 2 == 0, jnp.where(t == 0, 0.0, prev), jnp.where(t == n_tok - 1, 0.0, nxt))
    mixed = x + mu_ref[...] * (sh - x)

    def emit(val, dst):
        v4 = val.reshape(tt // 2, 2, n_b, LANES)
        even, odd = v4[:, 0], v4[:, 1]
        if n_b == SUBLANES:
            lo = lax.broadcasted_iota(jnp.int32, even.shape, 2) < R_N
            dst(0, jnp.where(lo, even, pltpu.roll(odd, R_N, 2)))
            dst(1, jnp.where(lo, pltpu.roll(even, R_N, 2), odd))
        else:
            dst(0, even)
            dst(1, odd)

    if not lora:
        (o_ref,) = rest

        def dst(row, val):
            o_ref[:, row] = val

        emit(mixed, dst)
    else:
        w_ref, o_ref = rest
        act = jnp.where(c < N_DIR, jnp.tanh(mixed), mixed)
        prod = jnp.dot(act.reshape(tt * n_b, LANES).astype(BF16), w_ref[...].astype(BF16),
                       preferred_element_type=F32)
        for hp in range(HEAD_PAIRS):
            def dst(row, val, hp=hp):
                o_ref[:, row, hp] = val

            emit(prod[:, hp * LANES:(hp + 1) * LANES].reshape(tt, n_b, LANES), dst)


def _rwkv_prep(p, r_mu, w_lora, grid_shift):
    T, B, _ = p.shape
    n_halo = T // GRID_W

    def specs(tt, col):
        per_blk = tt // GRID_W
        return [pl.BlockSpec((tt, B, LANES), lambda t, c: (t, 0, RS_BLK0 + col(c))),
                pl.BlockSpec((GRID_W, B, LANES),
                             lambda t, c: (jnp.maximum(t * per_blk - 1, 0), 0, RS_BLK0 + col(c))),
                pl.BlockSpec((GRID_W, B, LANES),
                             lambda t, c: (jnp.minimum((t + 1) * per_blk, n_halo - 1), 0, RS_BLK0 + col(c))),
                pl.BlockSpec((1, LANES), lambda t, c: (0, col(c)))]

    tt = min(MIX_TT, T)
    mix = pl.pallas_call(
        functools.partial(_prep_kernel, grid_shift=grid_shift, n_tok=T, lora=False),
        grid=(T // tt, MIX_SLOTS * HEAD_PAIRS),
        in_specs=specs(tt, lambda c: c),
        out_specs=pl.BlockSpec((None, tt // 2, 2, None, B, LANES),
                               lambda t, c: (c // HEAD_PAIRS, t, 0, c % HEAD_PAIRS, 0, 0)),
        out_shape=jax.ShapeDtypeStruct((MIX_SLOTS, T // 2, 2, HEAD_PAIRS, B, LANES), F32),
        compiler_params=_params(2),
        name="rwkv_prep",
    )(p, p, p, r_mu)
    tt = min(LORA_TT, T)
    lora = pl.pallas_call(
        functools.partial(_prep_kernel, grid_shift=grid_shift, n_tok=T, lora=True),
        grid=(T // tt, LORA_SLOTS),
        in_specs=specs(tt, lambda c: MIX_TILE0 + c // N_DIR)
        + [pl.BlockSpec((None, LANES, R_WIDTH), lambda t, c: (c, 0, 0))],
        out_specs=pl.BlockSpec((None, tt // 2, 2, HEAD_PAIRS, B, LANES),
                               lambda t, c: (c, t, 0, 0, 0, 0)),
        out_shape=jax.ShapeDtypeStruct((LORA_SLOTS, T // 2, 2, HEAD_PAIRS, B, LANES), F32),
        compiler_params=_params(2),
        name="rwkv_lora",
    )(p, p, p, r_mu, w_lora)
    return mix, lora


RWKV_TB = 32
J_UNROLL = 16
PREP_UNROLL = 4


def _chain_tiles(ref, pair, group, n_b):
    if n_b == SUBLANES:
        xt = ref[pair].reshape(LANES, LANES).T
        return [xt[:R_N], xt[R_N:]]
    tiles = []
    for parity in range(2):
        xt = ref[pair, parity].reshape(LANES, LANES).T
        tiles.append(jnp.where(group == 0, xt[:R_N], xt[R_N:]))
    return tiles


def _rwkv_kernel(r_ref, k_ref, v_ref, lw_ref, la_ref, pd_ref, ps_ref, s0_ref,
                 y_ref, c_ref, s_ref, kh_s, rh_s, vv_s, kt_s, bt_s, g_s, *, n_b):
    dirn = pl.program_id(0)
    group = pl.program_id(1)
    tb = pl.program_id(2)
    n_tok = y_ref.shape[0]
    n_pairs = n_tok // 2
    fwd = dirn == 0

    @pl.when(tb == 0)
    def _():
        s_ref[...] = s0_ref[...]

    w0, a0 = pd_ref[0], pd_ref[1]
    k_k, k_a, r_k = ps_ref[0], ps_ref[1], ps_ref[2]

    def prep(q, g):
        pair = q + dirn * (n_pairs - 1 - 2 * q)
        tiles = [_chain_tiles(ref, pair, group, n_b) for ref in (r_ref, k_ref, v_ref, lw_ref, la_ref)]
        for i in range(2):
            r, kraw, v, lw, la = (jnp.where(fwd, x[i], x[1 - i]) for x in tiles)
            pos = 2 * q + i
            tok = 2 * pair + i + dirn * (1 - 2 * i)
            wlog = -_softplus(-(w0 + lw)) - 0.5
            a = _sigmoid(a0 + la)
            kk = kraw * k_k
            kk = kk / jnp.maximum(jnp.sqrt(jnp.sum(kk * kk, axis=0, keepdims=True)), 1e-12)
            kz = kraw * (1.0 + (a - 1.0) * k_a)
            c_ref[tok] = jnp.sum(r * kz * r_k, axis=0, keepdims=True) * v
            kh_s[pos] = kk * g
            g = g * jnp.exp(-jnp.exp(wlog))
            g_inv = 1.0 / g
            vv_s[pos] = v
            bt_s[pos] = kk * a * g_inv
            kt_s[pos] = kz * g_inv
            rh_s[pos] = r * g
        return g

    g_s[...] = lax.fori_loop(0, n_pairs, prep, jnp.ones((R_N, LANES), F32), unroll=PREP_UNROLL)

    zeros = jnp.zeros((R_N, LANES), F32)

    def dot_first(jb, u):
        for jj in range(J_UNROLL):
            j = jb * J_UNROLL + jj
            u = u + s_ref[j] * kh_s[0, pl.ds(j, 1), :]
        return u

    def step(pos, u):
        tok = pos + dirn * (n_tok - 1 - 2 * pos)
        nxt = jnp.minimum(pos + 1, n_tok - 1)
        sa = -u
        vt = vv_s[pos]

        def update(jb, carry):
            y, u_next = carry
            for jj in range(J_UNROLL):
                j = jb * J_UNROLL + jj
                sn = s_ref[j] + sa * bt_s[pos, pl.ds(j, 1), :] + vt * kt_s[pos, pl.ds(j, 1), :]
                s_ref[j] = sn
                y = y + sn * rh_s[pos, pl.ds(j, 1), :]
                u_next = u_next + sn * kh_s[nxt, pl.ds(j, 1), :]
            return y, u_next

        y, u_next = lax.fori_loop(0, R_N // J_UNROLL, update, (zeros, zeros))
        y_ref[tok] = y
        return u_next

    lax.fori_loop(0, n_tok, step, lax.fori_loop(0, R_N // J_UNROLL, dot_first, zeros))

    def rescale(jb, carry):
        for jj in range(J_UNROLL):
            j = jb * J_UNROLL + jj
            s_ref[j] = s_ref[j] * g_s[pl.ds(j, 1), :]
        return carry

    lax.fori_loop(0, R_N // J_UNROLL, rescale, 0)


def _rwkv(xs, xl, p_dir, p_sh, s0):
    _, half_t, _, _, B, _ = xs.shape
    T = 2 * half_t
    G = p_sh.shape[0]
    tb = min(RWKV_TB, T)
    nt = T // tb
    rev = lambda dd, t: t + dd * (nt - 1 - 2 * t)

    def slot(s):
        return pl.BlockSpec((None, tb // 2, 2, HEAD_PAIRS, B, LANES),
                            lambda dd, g, t: (s(dd), rev(dd, t), 0, 0, 0, 0))

    state = pl.BlockSpec((None, None, R_N, R_N, LANES), lambda dd, g, t: (dd, g, 0, 0, 0))
    out = pl.BlockSpec((None, tb, R_N, LANES), lambda dd, g, t: (dd, rev(dd, t), 0, g))
    scratch = pltpu.VMEM((tb, R_N, LANES), F32)
    return pl.pallas_call(
        functools.partial(_rwkv_kernel, n_b=B),
        grid=(N_DIR, G, nt),
        in_specs=[slot(lambda dd: 0), slot(lambda dd: 1), slot(lambda dd: 2),
                  slot(lambda dd: dd), slot(lambda dd: N_DIR + dd),
                  pl.BlockSpec((None, None, 2, R_N, LANES), lambda dd, g, t: (dd, g, 0, 0, 0)),
                  pl.BlockSpec((None, 3, R_N, LANES), lambda dd, g, t: (g, 0, 0, 0)),
                  state],
        out_specs=[out, out, state],
        out_shape=[jax.ShapeDtypeStruct((N_DIR, T, R_N, G * LANES), F32),
                   jax.ShapeDtypeStruct((N_DIR, T, R_N, G * LANES), F32),
                   jax.ShapeDtypeStruct((N_DIR, G, R_N, R_N, LANES), F32)],
        scratch_shapes=[scratch] * 5 + [pltpu.VMEM((R_N, LANES), F32)],
        compiler_params=_params(3),
        name="rwkv_scan",
    )(xs, xs, xs, xl, xl, p_dir, p_sh, s0)


def _post_kernel(y_ref, c_ref, z_ref, ln_ref, o_ref, nat_s, *, n_b):
    n_tok = y_ref.shape[1]
    lo = lax.broadcasted_iota(jnp.int32, (R_N, LANES), 1) < R_N

    def normed(t, g):
        lanes = pl.ds(g * LANES, LANES)
        y = y_ref[0, t, :, lanes] + y_ref[1, t, :, lanes]
        mean = jnp.mean(y, axis=0, keepdims=True)
        yc = y - mean
        var = jnp.mean(yc * yc, axis=0, keepdims=True)
        return (yc * lax.rsqrt(var + LNX_EPS) * ln_ref[0, :, lanes] + ln_ref[1, :, lanes]
                + c_ref[0, t, :, lanes] + c_ref[1, t, :, lanes])

    rows_per_tok = HEAD_PAIRS * n_b

    def gate(t, x):
        z = z_ref[t]
        for hp in range(HEAD_PAIRS):
            r0 = pl.multiple_of(t * rows_per_tok + hp * n_b, SUBLANES)
            nat_s[pl.ds(r0, n_b), :] = x[hp * n_b:(hp + 1) * n_b] * _silu(z[:, hp * LANES:(hp + 1) * LANES])

    def body(tp, carry):
        t0 = tp * 2
        if n_b == SUBLANES:
            a, b = normed(t0, 0), normed(t0 + 1, 0)
            xt = jnp.concatenate([jnp.where(lo, a, pltpu.roll(b, R_N, 1)),
                                  jnp.where(lo, pltpu.roll(a, R_N, 1), b)], axis=0)
            x = xt.T
            for i in range(2):
                gate(t0 + i, x[i * R_N:(i + 1) * R_N])
        else:
            for i in range(2):
                gate(t0 + i, jnp.concatenate([normed(t0 + i, 0), normed(t0 + i, 1)], axis=0).T)
        return carry

    lax.fori_loop(0, n_tok // 2, body, 0, unroll=2)

    for hp in range(HEAD_PAIRS):
        for b in range(n_b):
            o_ref[b, :, hp * LANES:(hp + 1) * LANES] = nat_s[pl.ds(hp * n_b + b, n_tok, stride=rows_per_tok), :]


def _rwkv_post(y, c, p, ln):
    _, T, _, CH = y.shape
    B = p.shape[1]
    tb = min(RWKV_TB, T)
    chain = pl.BlockSpec((N_DIR, tb, R_N, CH), lambda t: (0, t, 0, 0))
    nat = pl.BlockSpec((tb, B, R_WIDTH), lambda t: (t, 0, RZ_BLK0))
    return pl.pallas_call(
        functools.partial(_post_kernel, n_b=B),
        grid=(T // tb,),
        in_specs=[chain, chain, nat, pl.BlockSpec((2, R_N, CH), lambda t: (0, 0, 0))],
        out_specs=pl.BlockSpec((B, tb, R_WIDTH), lambda t: (0, t, 0)),
        out_shape=jax.ShapeDtypeStruct((B, T, R_WIDTH), F32),
        scratch_shapes=[pltpu.VMEM((tb * HEAD_PAIRS * B, LANES), F32)],
        compiler_params=_params(1),
        name="rwkv_post",
    )(y, c, p, ln)


OUT_TM = 512


def _out_kernel(hm_ref, yr_ref, w_ref, x_ref, gate_ref, g_ref, o_ref):
    cat = jnp.concatenate([hm_ref[...], yr_ref[...].astype(BF16)], axis=1)
    out = jnp.dot(cat, w_ref[...], preferred_element_type=F32)
    z = x_ref[...] + gate_ref[...] * out
    o_ref[...] = z * lax.rsqrt(jnp.mean(z * z, axis=-1, keepdims=True) + EPS) * g_ref[...]


def _out_proj(hm, yr, w_out, x, gate, final_g):
    B, T, D = x.shape
    tm = min(OUT_TM, T)
    per_b = gate.shape[0] == B
    row = lambda b, t: (b, t, 0)
    return pl.pallas_call(
        _out_kernel,
        grid=(B, T // tm),
        in_specs=[pl.BlockSpec((None, tm, M_WIDTH), row),
                  pl.BlockSpec((None, tm, R_WIDTH), row),
                  pl.BlockSpec((D, D), lambda b, t: (0, 0)),
                  pl.BlockSpec((None, tm, D), row),
                  pl.BlockSpec((None, 1, D), lambda b, t: (b if per_b else 0, 0, 0)),
                  pl.BlockSpec((1, D), lambda b, t: (0, 0))],
        out_specs=pl.BlockSpec((None, tm, D), row),
        out_shape=jax.ShapeDtypeStruct((B, T, D), F32),
        compiler_params=_params(2),
        name="out_proj",
    )(hm, yr, w_out, x, gate, final_g)


def _chain_param(p, n_b):
    q = p.reshape(HEAD_PAIRS, 2, R_N)
    if n_b == SUBLANES:
        t = jnp.broadcast_to(q.transpose(2, 1, 0)[..., None], (R_N, 2, HEAD_PAIRS, n_b))
        return t.reshape(1, R_N, LANES)
    t = jnp.broadcast_to(q.transpose(1, 2, 0)[..., None], (2, R_N, HEAD_PAIRS, n_b))
    return t.reshape(2, R_N, LANES)


def _chain_state(s, n_b):
    s6 = s.reshape(n_b, N_DIR, HEAD_PAIRS, 2, R_N, R_N)
    if n_b == SUBLANES:
        return s6.transpose(1, 5, 4, 3, 2, 0).reshape(N_DIR, 1, R_N, R_N, LANES)
    return s6.transpose(1, 3, 5, 4, 2, 0).reshape(N_DIR, 2, R_N, R_N, LANES)


def _unchain_state(s, n_b):
    if n_b == SUBLANES:
        s6 = s.reshape(N_DIR, R_N, R_N, 2, HEAD_PAIRS, n_b).transpose(5, 0, 4, 3, 2, 1)
    else:
        s6 = s.reshape(N_DIR, 2, R_N, R_N, HEAD_PAIRS, n_b).transpose(5, 0, 4, 1, 3, 2)
    return s6.reshape(n_b, N_DIR, R_HEADS, R_N, R_N)


def _block(x, mod, st, lw, norm_g, final_g, grid):
    (w_mproj, w_rproj, m_conv_w, m_conv_b, m_gate_b, m_ln_g, r_mu, r_w0, w_lora, r_a0,
     r_k_k, r_k_a, r_r_k, r_ln_g, r_ln_b, w_out) = lw
    C0, n0, m0, S0 = st
    B, T, D = x.shape
    assert B in (SUBLANES, 2 * SUBLANES), "chain layout is written for 8 or 16 sequences"
    shift, scale, gate = jnp.split(mod, 3, axis=-1)
    g2 = norm_g.reshape(1, D)
    p = _proj(x, scale, shift, g2, w_mproj, M_PROJ_COLS // 3, False).reshape(B, T, M_PROJ_COLS)
    pr = _proj(x, scale, shift, g2, w_rproj, R_PROJ_COLS // 2, True)
    pr = pr.reshape(T, B, R_PROJ_COLS)
    mg = p[..., 5 * M_WIDTH:5 * M_WIDTH + M_GATE_COLS]

    gates = (mg.reshape(B, T, N_DIR, 2, M_HEADS) + m_gate_b).transpose(2, 3, 0, 4, 1)
    nc = T // CHUNK
    log_i = gates[:, 0].reshape(N_DIR, B, M_HEADS, nc, CHUNK)
    log_f = jax.nn.log_sigmoid(gates[:, 1]).reshape(N_DIR, B, M_HEADS, nc, CHUNK)
    hm, Cn, nn, mn = _mlstm(
        p, m_conv_w, m_conv_b.reshape(1, 2 * M_WIDTH), m_ln_g.reshape(1, M_WIDTH), log_i, log_f,
        C0, n0[..., None, :], jnp.broadcast_to(m0[..., None, None], m0.shape + (1, LANES)))

    xs, xl = _rwkv_prep(pr, r_mu.reshape(1, SHIFT_COLS), w_lora, grid)
    p_dir = jnp.stack([jnp.stack([_chain_param(r_w0[z], B), _chain_param(r_a0[z], B)], axis=1)
                       for z in range(N_DIR)])
    p_sh = jnp.stack([_chain_param(r_k_k, B), _chain_param(r_k_a, B),
                      _chain_param(r_r_k.reshape(R_WIDTH), B)], axis=1)
    y_dirs, c_dirs, S_new = _rwkv(xs, xl, p_dir, p_sh, _chain_state(S0, B))
    G = p_sh.shape[0]
    ln = jnp.stack([_chain_param(r_ln_g, B), _chain_param(r_ln_b, B)])
    ln = ln.transpose(0, 2, 1, 3).reshape(2, R_N, G * LANES)
    yr = _rwkv_post(y_dirs, c_dirs, pr, ln)

    out = _out_proj(hm, yr, w_out, x, gate, final_g.reshape(1, D))
    return out, (Cn, nn[..., 0, :], mn[..., 0, 0], _unchain_state(S_new, B))


def kernel(x_prompt, x_sample, state_mlstm_C, state_mlstm_n, state_mlstm_m, state_rwkv_S, c, c_ctx,
           norm_g, w_ada, b_ada, w_in, m_conv_w, m_conv_b, m_gate_b, m_ln_g, r_mu, r_w0, r_w2, r_a0,
           r_a2, r_k_k, r_k_a, r_r_k, r_ln_g, r_ln_b, w_out, final_g):
    depth = w_in.shape[0]
    assert depth == 1, "the final RMSNorm is fused into the single layer's output projection"
    bp = x_prompt.shape[0]
    ctx_state0 = (jnp.zeros((bp, N_DIR, M_HEADS, M_DK, M_DK), F32),
                  jnp.zeros((bp, N_DIR, M_HEADS, M_DK), F32),
                  jnp.full((bp, N_DIR, M_HEADS), -jnp.inf, F32),
                  jnp.zeros((bp, N_DIR, R_HEADS, R_N, R_N), F32))
    l = 0
    w_l = w_in[l]
    gate_lo = 5 * M_WIDTH
    gate_hi = gate_lo + M_GATE_COLS
    w_mproj = jnp.concatenate(
        [w_l[:, :gate_hi], jnp.zeros((D_MODEL, GATE_PAD - M_GATE_COLS), F32)], axis=1).astype(BF16)
    w_rproj = w_l[:, gate_hi:].astype(BF16)
    zpad = jnp.zeros((LORA, R_WIDTH), F32)
    w_lora = jnp.stack([jnp.concatenate([r_w2[l, 0], zpad]), jnp.concatenate([zpad, r_w2[l, 1]]),
                        jnp.concatenate([r_a2[l, 0], zpad]), jnp.concatenate([zpad, r_a2[l, 1]])])
    lw = (w_mproj, w_rproj, m_conv_w[l], m_conv_b[l], m_gate_b[l], m_ln_g[l], r_mu[l], r_w0[l], w_lora,
          r_a0[l], r_k_k[l], r_k_a[l], r_r_k[l], r_ln_g[l], r_ln_b[l], w_out[l].astype(BF16))
    n_cond = 1 + c.shape[0]
    cond = jnp.concatenate([c_ctx[None, :], c, jnp.zeros((16 - n_cond, D_MODEL), F32)], axis=0)
    mod = _mm(jax.nn.silu(cond), w_ada[l], 16, 1536)[:n_cond] + b_ada[l]
    mod_p = mod[0:1, None, :]
    mod_s = mod[1:, None, :]
    y_prompt, (Cp, np_, mp, Sp) = _block(x_prompt, mod_p, ctx_state0, lw, norm_g[l], final_g, False)
    st_s = (state_mlstm_C[:, l], state_mlstm_n[:, l], state_mlstm_m[:, l], state_rwkv_S[:, l])
    y_sample, _ = _block(x_sample, mod_s, st_s, lw, norm_g[l], final_g, True)
    return (y_prompt, y_sample, Cp[:, None], np_[:, None], mp[:, None], Sp[:, None])
```

```python
import functools

import jax
import jax.numpy as jnp
import numpy as np
from jax import lax
from jax.experimental import pallas as pl
from jax.experimental.pallas import tpu as pltpu

D_MODEL = 2048
GRID_W = 64
N_DIR = 2
M_WIDTH = D_MODEL // 2
M_HEADS = 4
M_DK = M_WIDTH // M_HEADS
R_WIDTH = D_MODEL - M_WIDTH
R_N = 64
R_HEADS = R_WIDTH // R_N
LORA = 64
CONV_K = 3
CHUNK = 64
EPS = 1e-6
LNX_EPS = 64e-5
M_GATE_COLS = N_DIR * 2 * M_HEADS
SHIFT_COLS = 3 * R_WIDTH + 2 * N_DIR * LORA

LANES = 128
SUBLANES = 8
HEAD_PAIRS = R_WIDTH // LANES
GATE_PAD = 2 * LANES
M_PROJ_COLS = 5 * M_WIDTH + GATE_PAD
R_PROJ_COLS = R_WIDTH + SHIFT_COLS
RZ_BLK0 = 0
RS_BLK0 = R_WIDTH // LANES
VMEM_LIMIT = 48 * 1024 * 1024

F32 = jnp.float32
BF16 = jnp.bfloat16


def _params(n_axes, vmem_limit=VMEM_LIMIT):
    return pltpu.CompilerParams(dimension_semantics=("arbitrary",) * n_axes,
                                vmem_limit_bytes=vmem_limit)


def _sigmoid(x):
    return 1.0 / (1.0 + jnp.exp(-x))


def _silu(x):
    return x * _sigmoid(x)


def _softplus(x):
    return jnp.maximum(x, 0.0) + jnp.log(1.0 + jnp.exp(-jnp.abs(x)))


def _mm_kernel(x_ref, w_ref, o_ref):
    o_ref[...] = jnp.dot(x_ref[...].astype(BF16), w_ref[...].astype(BF16),
                         preferred_element_type=F32)


def _mm(x, w, tm, tn):
    M, K = x.shape
    N = w.shape[1]
    assert M % tm == 0 and N % tn == 0
    return pl.pallas_call(
        _mm_kernel,
        grid=(N // tn, M // tm),
        in_specs=[pl.BlockSpec((tm, K), lambda j, i: (i, 0)),
                  pl.BlockSpec((K, tn), lambda j, i: (0, j))],
        out_specs=pl.BlockSpec((tm, tn), lambda j, i: (i, j)),
        out_shape=jax.ShapeDtypeStruct((M, N), F32),
        compiler_params=_params(2),
        name="proj_mm",
    )(x, w)


PROJ_ROWS = 512


def _modulated(x, g, scale, shift):
    return x * lax.rsqrt(jnp.mean(x * x, axis=-1, keepdims=True) + EPS) * g * (1.0 + scale) + shift


def _proj_kernel(x_ref, sc_ref, sh_ref, g_ref, w_ref, o_ref):
    h = _modulated(x_ref[...], g_ref[...], sc_ref[...], sh_ref[...])
    o_ref[...] = jnp.dot(h.astype(BF16), w_ref[...], preferred_element_type=F32)


def _proj(x, scale, shift, norm_g, w, tn):
    B, T, D = x.shape
    N = w.shape[1]
    per_b = scale.shape[0] == B
    rows = min(PROJ_ROWS, T) if per_b else min(PROJ_ROWS, B * T)
    mod_spec = pl.BlockSpec((None, 1, D), lambda j, i: (i * rows // T if per_b else 0, 0, 0))
    return pl.pallas_call(
        _proj_kernel,
        grid=(N // tn, B * T // rows),
        in_specs=[pl.BlockSpec((rows, D), lambda j, i: (i, 0)), mod_spec, mod_spec,
                  pl.BlockSpec((1, D), lambda j, i: (0, 0)),
                  pl.BlockSpec((D, tn), lambda j, i: (0, j))],
        out_specs=pl.BlockSpec((rows, tn), lambda j, i: (i, j)),
        out_shape=jax.ShapeDtypeStruct((B * T, N), F32),
        compiler_params=_params(2),
        name="proj_bm",
    )(x.reshape(B * T, D), scale, shift, norm_g, w)


MLSTM_UNROLL = 4
MLSTM_VMEM_LIMIT = 56 * 1024 * 1024


def _mlstm_kernel(q_ref, k_ref, v_ref, o_ref, z_ref, wq_ref, wk_ref, bq_ref, bk_ref, lng_ref,
                  gi_ref, gf_ref, c0_ref, n0_ref, m0_ref,
                  hm_ref, c_ref, n_ref, m_ref, qs, ks, hs_f, hs_b, bcol_s, icol_s, rmax_s, rows_s):
    n_tok = q_ref.shape[0]
    nc = n_tok // CHUNK
    L = CHUNK
    tpos = lax.broadcasted_iota(jnp.int32, (n_tok, M_DK), 0)

    def conv_silu(x_ref, w_ref, b_ref):
        x = x_ref[...]
        prev = jnp.where(tpos == 0, 0.0, pltpu.roll(x, 1, 0))
        nxt = jnp.where(tpos == n_tok - 1, 0.0, pltpu.roll(x, n_tok - 1, 0))
        return _silu(prev * w_ref[0:1, :] + x * w_ref[1:2, :] + nxt * w_ref[2:3, :] + b_ref[...])

    qs[...] = conv_silu(q_ref, wq_ref, bq_ref)
    ks[...] = conv_silu(k_ref, wk_ref, bk_ref) * (M_DK ** -0.5)
    c_ref[...] = c0_ref[...]

    row = lax.broadcasted_iota(jnp.int32, (L, L), 0)
    col = lax.broadcasted_iota(jnp.int32, (L, L), 1)
    eye = row == col
    neg_inf = jnp.float32(-jnp.inf)

    def masks(dirn):
        seen = col <= row if dirn == 0 else col >= row
        seen_t = row <= col if dirn == 0 else row >= col
        return seen, seen_t

    def gate_stats(dirn, cc):
        seen, seen_t = masks(dirn)
        idx = dirn * nc + cc
        f_row = gf_ref[dirn, pl.ds(cc, 1), :]
        i_row = gi_ref[dirn, pl.ds(cc, 1), :]
        f_col = jnp.sum(jnp.where(eye, f_row, 0.0), axis=1, keepdims=True)
        i_col = jnp.sum(jnp.where(eye, i_row, 0.0), axis=1, keepdims=True)
        b_col = jnp.sum(jnp.where(seen, f_row, 0.0), axis=1, keepdims=True)
        b_row = jnp.sum(jnp.where(seen_t, f_col, 0.0), axis=0, keepdims=True)
        dmat = jnp.where(seen, b_col - b_row + i_row, neg_inf)
        b_last = jnp.sum(f_row, axis=1, keepdims=True)
        g_row = b_last - b_row + i_row
        bcol_s[idx] = jnp.broadcast_to(b_col, (L, LANES))
        icol_s[idx] = jnp.broadcast_to(i_col, (L, LANES))
        rmax_s[idx] = jnp.broadcast_to(jnp.max(dmat, axis=1, keepdims=True), (L, LANES))
        rows_s[idx, 0:1, 0:L] = b_row
        rows_s[idx, 1:2, 0:L] = i_row
        rows_s[idx, 2:3, :] = jnp.broadcast_to(b_last, (1, LANES))
        rows_s[idx, 3:4, :] = jnp.broadcast_to(jnp.max(g_row, axis=1, keepdims=True), (1, LANES))

    def stats_body(cc, carry):
        gate_stats(0, cc)
        gate_stats(1, cc)
        return carry

    lax.fori_loop(0, nc, stats_body, 0, unroll=min(4, nc))

    def wide(x):
        return jnp.concatenate([x] * (M_DK // LANES), axis=1)

    def chunk(dirn, cc, n, m):
        seen, _ = masks(dirn)
        idx = dirn * nc + cc
        r0 = pl.multiple_of(cc * L, L)
        q = qs[pl.ds(r0, L), :]
        k = ks[pl.ds(r0, L), :]
        v = v_ref[pl.ds(r0, L), :]
        b_col = bcol_s[idx]
        b_row = rows_s[idx, 0:1, 0:L]
        i_row = rows_s[idx, 1:2, 0:L]
        b_last = rows_s[idx, 2:3, :]
        inter = b_col + m
        mt = jnp.maximum(inter, rmax_s[idx])
        dmat = jnp.where(seen, b_col[:, :L] - b_row + i_row, neg_inf)
        qb = q.astype(BF16)
        kb = k.astype(BF16)
        vb = v.astype(BF16)
        qk = lax.dot_general(qb, kb, (((1,), (1,)), ((), ())), preferred_element_type=F32)
        a = jnp.exp(dmat - mt[:, :L]) * qk
        s_in = jnp.exp(inter - mt)
        c = c_ref[dirn]
        qc = jnp.dot(qb, c.astype(BF16), preferred_element_type=F32)
        av = jnp.dot(a.astype(BF16), vb, preferred_element_type=F32)
        num = wide(s_in) * qc + av
        den = s_in * jnp.sum(q * n, axis=1, keepdims=True) + jnp.sum(a, axis=1, keepdims=True)
        hs = hs_f if dirn == 0 else hs_b
        hs[pl.ds(r0, L), :] = num / wide(jnp.maximum(jnp.abs(den), jnp.exp(-mt)))
        m_new = jnp.maximum(b_last + m, rows_s[idx, 3:4, :])
        decay = wide(jnp.exp(b_last + m - m_new))
        g_col = b_last - b_col + icol_s[idx]
        wk = wide(jnp.exp(g_col - m_new)) * k
        c_ref[dirn] = decay * c + lax.dot_general(
            wk.astype(BF16), vb, (((0,), (0,)), ((), ())), preferred_element_type=F32)
        n_new = decay * n + jnp.sum(wk, axis=0, keepdims=True)
        return n_new, m_new

    def body(ci, carry):
        n_f, m_f, n_b, m_b = carry
        n_f, m_f = chunk(0, ci, n_f, m_f)
        n_b, m_b = chunk(1, nc - 1 - ci, n_b, m_b)
        return n_f, m_f, n_b, m_b

    n_f, m_f, n_b, m_b = lax.fori_loop(
        0, nc, body, (n0_ref[0], m0_ref[0], n0_ref[1], m0_ref[1]), unroll=MLSTM_UNROLL)
    n_ref[0] = n_f
    n_ref[1] = n_b
    m_ref[0] = m_f
    m_ref[1] = m_b

    hm = _sigmoid(o_ref[...]) * (hs_f[...] + hs_b[...])
    hc = hm - jnp.mean(hm, axis=1, keepdims=True)
    hn = hc * lax.rsqrt(jnp.mean(hc * hc, axis=1, keepdims=True) + EPS)
    hm_ref[...] = (hn * lng_ref[...] * _silu(z_ref[...])).astype(BF16)


def _mlstm(p, conv_w, conv_b, ln_g, gi, gf, c0, n0, m0):
    B, T, _ = p.shape
    nc = T // CHUNK
    tiles_per_part = M_WIDTH // M_DK

    def col_spec(part):
        return pl.BlockSpec((None, T, M_DK), lambda b, h: (b, 0, part * tiles_per_part + h))

    def par_spec(rows, part):
        return pl.BlockSpec((rows, M_DK), lambda b, h: (0, part * tiles_per_part + h))

    gate_spec = pl.BlockSpec((N_DIR, None, None, nc, CHUNK), lambda b, h: (0, b, h, 0, 0))
    c_spec = pl.BlockSpec((None, N_DIR, None, M_DK, M_DK), lambda b, h: (b, 0, h, 0, 0))
    n_spec = pl.BlockSpec((None, N_DIR, None, 1, M_DK), lambda b, h: (b, 0, h, 0, 0))
    m_spec = pl.BlockSpec((None, N_DIR, None, 1, LANES), lambda b, h: (b, 0, h, 0, 0))
    seq = pltpu.VMEM((T, M_DK), F32)
    stat = pltpu.VMEM((N_DIR * nc, CHUNK, LANES), F32)
    return pl.pallas_call(
        _mlstm_kernel,
        grid=(B, M_HEADS),
        in_specs=[col_spec(0), col_spec(1), col_spec(2), col_spec(3), col_spec(4),
                  par_spec(CONV_K, 0), par_spec(CONV_K, 1), par_spec(1, 0), par_spec(1, 1),
                  par_spec(1, 0), gate_spec, gate_spec, c_spec, n_spec, m_spec],
        out_specs=[pl.BlockSpec((None, T, M_DK), lambda b, h: (b, 0, h)), c_spec, n_spec, m_spec],
        out_shape=[jax.ShapeDtypeStruct((B, T, M_WIDTH), BF16),
                   jax.ShapeDtypeStruct((B, N_DIR, M_HEADS, M_DK, M_DK), F32),
                   jax.ShapeDtypeStruct((B, N_DIR, M_HEADS, 1, M_DK), F32),
                   jax.ShapeDtypeStruct((B, N_DIR, M_HEADS, 1, LANES), F32)],
        scratch_shapes=[seq, seq, seq, seq, stat, stat, stat,
                        pltpu.VMEM((N_DIR * nc, SUBLANES, LANES), F32)],
        compiler_params=_params(2, MLSTM_VMEM_LIMIT),
        name="mlstm_scan",
    )(p, p, p, p, p, conv_w, conv_w, conv_b, conv_b, ln_g, gi, gf, c0, n0, m0)


MIX_SLOTS = 3
LORA_SLOTS = 2 * N_DIR
MIX_TT = 512
LORA_TT = 128
MIX_TILE0 = SHIFT_COLS // LANES - 2


def _prep_kernel(x_ref, lo_ref, hi_ref, mu_ref, *rest, grid_shift, n_tok, lora):
    tblk = pl.program_id(0)
    c = pl.program_id(1)
    x = x_ref[...]
    tt, n_b, _ = x.shape
    t = tblk * tt + lax.broadcasted_iota(jnp.int32, x.shape, 0)
    lane4 = lax.broadcasted_iota(jnp.int32, x.shape, 2) % 4
    xe = jnp.concatenate([lo_ref[...], x, hi_ref[...]], axis=0)
    prev = xe[GRID_W - 1:GRID_W - 1 + tt]
    nxt = xe[GRID_W + 1:GRID_W + 1 + tt]
    if grid_shift:
        col = t % GRID_W
        left = jnp.where(col == 0, 0.0, prev)
        right = jnp.where(col == GRID_W - 1, 0.0, nxt)
        up = jnp.where(t < GRID_W, 0.0, xe[:tt])
        down = jnp.where(t >= n_tok - GRID_W, 0.0, xe[2 * GRID_W:])
        sh = jnp.where(lane4 == 0, left, jnp.where(lane4 == 1, right, jnp.where(lane4 == 2, up, down)))
    else:
        sh = jnp.where(lane4 % 2 == 0, jnp.where(t == 0, 0.0, prev), jnp.where(t == n_tok - 1, 0.0, nxt))
    mixed = x + mu_ref[...] * (sh - x)

    def emit(val, dst):
        v4 = val.reshape(tt // 2, 2, n_b, LANES)
        even, odd = v4[:, 0], v4[:, 1]
        if n_b == SUBLANES:
            lo = lax.broadcasted_iota(jnp.int32, even.shape, 2) < R_N
            dst(0, jnp.where(lo, even, pltpu.roll(odd, R_N, 2)))
            dst(1, jnp.where(lo, pltpu.roll(even, R_N, 2), odd))
        else:
            dst(0, even)
            dst(1, odd)

    if not lora:
        (o_ref,) = rest

        def dst(row, val):
            o_ref[:, row] = val

        emit(mixed, dst)
    else:
        w_ref, o_ref = rest
        act = jnp.where(c < N_DIR, jnp.tanh(mixed), mixed)
        prod = jnp.dot(act.reshape(tt * n_b, LANES).astype(BF16), w_ref[...].astype(BF16),
                       preferred_element_type=F32)
        for hp in range(HEAD_PAIRS):
            def dst(row, val, hp=hp):
                o_ref[:, row, hp] = val

            emit(prod[:, hp * LANES:(hp + 1) * LANES].reshape(tt, n_b, LANES), dst)


def _rwkv_prep(p, r_mu, w_lora, grid_shift):
    T, B, _ = p.shape
    n_halo = T // GRID_W

    def specs(tt, col):
        per_blk = tt // GRID_W
        return [pl.BlockSpec((tt, B, LANES), lambda t, c: (t, 0, RS_BLK0 + col(c))),
                pl.BlockSpec((GRID_W, B, LANES),
                             lambda t, c: (jnp.maximum(t * per_blk - 1, 0), 0, RS_BLK0 + col(c))),
                pl.BlockSpec((GRID_W, B, LANES),
                             lambda t, c: (jnp.minimum((t + 1) * per_blk, n_halo - 1), 0, RS_BLK0 + col(c))),
                pl.BlockSpec((1, LANES), lambda t, c: (0, col(c)))]

    tt = min(MIX_TT, T)
    mix = pl.pallas_call(
        functools.partial(_prep_kernel, grid_shift=grid_shift, n_tok=T, lora=False),
        grid=(T // tt, MIX_SLOTS * HEAD_PAIRS),
        in_specs=specs(tt, lambda c: c),
        out_specs=pl.BlockSpec((None, tt // 2, 2, None, B, LANES),
                               lambda t, c: (c // HEAD_PAIRS, t, 0, c % HEAD_PAIRS, 0, 0)),
        out_shape=jax.ShapeDtypeStruct((MIX_SLOTS, T // 2, 2, HEAD_PAIRS, B, LANES), F32),
        compiler_params=_params(2),
        name="rwkv_prep",
    )(p, p, p, r_mu)
    tt = min(LORA_TT, T)
    lora = pl.pallas_call(
        functools.partial(_prep_kernel, grid_shift=grid_shift, n_tok=T, lora=True),
        grid=(T // tt, LORA_SLOTS),
        in_specs=specs(tt, lambda c: MIX_TILE0 + c // N_DIR)
        + [pl.BlockSpec((None, LANES, R_WIDTH), lambda t, c: (c, 0, 0))],
        out_specs=pl.BlockSpec((None, tt // 2, 2, HEAD_PAIRS, B, LANES),
                               lambda t, c: (c, t, 0, 0, 0, 0)),
        out_shape=jax.ShapeDtypeStruct((LORA_SLOTS, T // 2, 2, HEAD_PAIRS, B, LANES), F32),
        compiler_params=_params(2),
        name="rwkv_lora",
    )(p, p, p, r_mu, w_lora)
    return mix, lora


RWKV_TB = 32
J_UNROLL = 16
PREP_UNROLL = 4


def _chain_tiles(ref, pair, group, n_b):
    if n_b == SUBLANES:
        xt = ref[pair].reshape(LANES, LANES).T
        return [xt[:R_N], xt[R_N:]]
    tiles = []
    for parity in range(2):
        xt = ref[pair, parity].reshape(LANES, LANES).T
        tiles.append(jnp.where(group == 0, xt[:R_N], xt[R_N:]))
    return tiles


def _rwkv_kernel(r_ref, k_ref, v_ref, lw_ref, la_ref, pd_ref, ps_ref, s0_ref,
                 y_ref, c_ref, s_ref, kh_s, rh_s, vv_s, kt_s, bt_s, g_s, *, n_b):
    dirn = pl.program_id(0)
    group = pl.program_id(1)
    tb = pl.program_id(2)
    n_tok = y_ref.shape[0]
    n_pairs = n_tok // 2
    fwd = dirn == 0

    @pl.when(tb == 0)
    def _():
        s_ref[...] = s0_ref[...]

    w0, a0 = pd_ref[0], pd_ref[1]
    k_k, k_a, r_k = ps_ref[0], ps_ref[1], ps_ref[2]

    def prep(q, g):
        pair = q + dirn * (n_pairs - 1 - 2 * q)
        tiles = [_chain_tiles(ref, pair, group, n_b) for ref in (r_ref, k_ref, v_ref, lw_ref, la_ref)]
        for i in range(2):
            r, kraw, v, lw, la = (jnp.where(fwd, x[i], x[1 - i]) for x in tiles)
            pos = 2 * q + i
            tok = 2 * pair + i + dirn * (1 - 2 * i)
            wlog = -_softplus(-(w0 + lw)) - 0.5
            a = _sigmoid(a0 + la)
            kk = kraw * k_k
            kk = kk / jnp.maximum(jnp.sqrt(jnp.sum(kk * kk, axis=0, keepdims=True)), 1e-12)
            kz = kraw * (1.0 + (a - 1.0) * k_a)
            c_ref[tok] = jnp.sum(r * kz * r_k, axis=0, keepdims=True) * v
            kh_s[pos] = kk * g
            g = g * jnp.exp(-jnp.exp(wlog))
            g_inv = 1.0 / g
            vv_s[pos] = v
            bt_s[pos] = kk * a * g_inv
            kt_s[pos] = kz * g_inv
            rh_s[pos] = r * g
        return g

    g_s[...] = lax.fori_loop(0, n_pairs, prep, jnp.ones((R_N, LANES), F32), unroll=PREP_UNROLL)

    zeros = jnp.zeros((R_N, LANES), F32)

    def dot_first(jb, u):
        for jj in range(J_UNROLL):
            j = jb * J_UNROLL + jj
            u = u + s_ref[j] * kh_s[0, pl.ds(j, 1), :]
        return u

    def step(pos, u):
        tok = pos + dirn * (n_tok - 1 - 2 * pos)
        nxt = jnp.minimum(pos + 1, n_tok - 1)
        sa = -u
        vt = vv_s[pos]

        def update(jb, carry):
            y, u_next = carry
            for jj in range(J_UNROLL):
                j = jb * J_UNROLL + jj
                sn = s_ref[j] + sa * bt_s[pos, pl.ds(j, 1), :] + vt * kt_s[pos, pl.ds(j, 1), :]
                s_ref[j] = sn
                y = y + sn * rh_s[pos, pl.ds(j, 1), :]
                u_next = u_next + sn * kh_s[nxt, pl.ds(j, 1), :]
            return y, u_next

        y, u_next = lax.fori_loop(0, R_N // J_UNROLL, update, (zeros, zeros))
        y_ref[tok] = y
        return u_next

    lax.fori_loop(0, n_tok, step, lax.fori_loop(0, R_N // J_UNROLL, dot_first, zeros))

    def rescale(jb, carry):
        for jj in range(J_UNROLL):
            j = jb * J_UNROLL + jj
            s_ref[j] = s_ref[j] * g_s[pl.ds(j, 1), :]
        return carry

    lax.fori_loop(0, R_N // J_UNROLL, rescale, 0)


def _rwkv(xs, xl, p_dir, p_sh, s0):
    _, half_t, _, _, B, _ = xs.shape
    T = 2 * half_t
    G = p_sh.shape[0]
    tb = min(RWKV_TB, T)
    nt = T // tb
    rev = lambda dd, t: t + dd * (nt - 1 - 2 * t)

    def slot(s):
        return pl.BlockSpec((None, tb // 2, 2, HEAD_PAIRS, B, LANES),
                            lambda dd, g, t: (s(dd), rev(dd, t), 0, 0, 0, 0))

    state = pl.BlockSpec((None, None, R_N, R_N, LANES), lambda dd, g, t: (dd, g, 0, 0, 0))
    out = pl.BlockSpec((None, tb, R_N, LANES), lambda dd, g, t: (dd, rev(dd, t), 0, g))
    scratch = pltpu.VMEM((tb, R_N, LANES), F32)
    return pl.pallas_call(
        functools.partial(_rwkv_kernel, n_b=B),
        grid=(N_DIR, G, nt),
        in_specs=[slot(lambda dd: 0), slot(lambda dd: 1), slot(lambda dd: 2),
                  slot(lambda dd: dd), slot(lambda dd: N_DIR + dd),
                  pl.BlockSpec((None, None, 2, R_N, LANES), lambda dd, g, t: (dd, g, 0, 0, 0)),
                  pl.BlockSpec((None, 3, R_N, LANES), lambda dd, g, t: (g, 0, 0, 0)),
                  state],
        out_specs=[out, out, state],
        out_shape=[jax.ShapeDtypeStruct((N_DIR, T, R_N, G * LANES), F32),
                   jax.ShapeDtypeStruct((N_DIR, T, R_N, G * LANES), F32),
                   jax.ShapeDtypeStruct((N_DIR, G, R_N, R_N, LANES), F32)],
        scratch_shapes=[scratch] * 5 + [pltpu.VMEM((R_N, LANES), F32)],
        compiler_params=_params(3),
        name="rwkv_scan",
    )(xs, xs, xs, xl, xl, p_dir, p_sh, s0)


def _post_kernel(y_ref, c_ref, z_ref, ln_ref, o_ref, nat_s, *, n_b):
    n_tok = y_ref.shape[1]
    lo = lax.broadcasted_iota(jnp.int32, (R_N, LANES), 1) < R_N

    def normed(t, g):
        lanes = pl.ds(g * LANES, LANES)
        y = y_ref[0, t, :, lanes] + y_ref[1, t, :, lanes]
        mean = jnp.mean(y, axis=0, keepdims=True)
        yc = y - mean
        var = jnp.mean(yc * yc, axis=0, keepdims=True)
        return (yc * lax.rsqrt(var + LNX_EPS) * ln_ref[0, :, lanes] + ln_ref[1, :, lanes]
                + c_ref[0, t, :, lanes] + c_ref[1, t, :, lanes])

    rows_per_tok = HEAD_PAIRS * n_b

    def gate(t, x):
        z = z_ref[t]
        for hp in range(HEAD_PAIRS):
            r0 = pl.multiple_of(t * rows_per_tok + hp * n_b, SUBLANES)
            nat_s[pl.ds(r0, n_b), :] = x[hp * n_b:(hp + 1) * n_b] * _silu(z[:, hp * LANES:(hp + 1) * LANES])

    def body(tp, carry):
        t0 = tp * 2
        if n_b == SUBLANES:
            a, b = normed(t0, 0), normed(t0 + 1, 0)
            xt = jnp.concatenate([jnp.where(lo, a, pltpu.roll(b, R_N, 1)),
                                  jnp.where(lo, pltpu.roll(a, R_N, 1), b)], axis=0)
            x = xt.T
            for i in range(2):
                gate(t0 + i, x[i * R_N:(i + 1) * R_N])
        else:
            for i in range(2):
                gate(t0 + i, jnp.concatenate([normed(t0 + i, 0), normed(t0 + i, 1)], axis=0).T)
        return carry

    lax.fori_loop(0, n_tok // 2, body, 0, unroll=2)

    for hp in range(HEAD_PAIRS):
        for b in range(n_b):
            o_ref[b, :, hp * LANES:(hp + 1) * LANES] = nat_s[pl.ds(hp * n_b + b, n_tok, stride=rows_per_tok), :]


def _rwkv_post(y, c, p, ln):
    _, T, _, CH = y.shape
    B = p.shape[1]
    tb = min(RWKV_TB, T)
    chain = pl.BlockSpec((N_DIR, tb, R_N, CH), lambda t: (0, t, 0, 0))
    nat = pl.BlockSpec((tb, B, R_WIDTH), lambda t: (t, 0, RZ_BLK0))
    return pl.pallas_call(
        functools.partial(_post_kernel, n_b=B),
        grid=(T // tb,),
        in_specs=[chain, chain, nat, pl.BlockSpec((2, R_N, CH), lambda t: (0, 0, 0))],
        out_specs=pl.BlockSpec((B, tb, R_WIDTH), lambda t: (0, t, 0)),
        out_shape=jax.ShapeDtypeStruct((B, T, R_WIDTH), F32),
        scratch_shapes=[pltpu.VMEM((tb * HEAD_PAIRS * B, LANES), F32)],
        compiler_params=_params(1),
        name="rwkv_post",
    )(y, c, p, ln)


OUT_TM = 512


def _out_kernel(hm_ref, yr_ref, w_ref, x_ref, gate_ref, g_ref, o_ref):
    cat = jnp.concatenate([hm_ref[...], yr_ref[...].astype(BF16)], axis=1)
    out = jnp.dot(cat, w_ref[...], preferred_element_type=F32)
    z = x_ref[...] + gate_ref[...] * out
    o_ref[...] = z * lax.rsqrt(jnp.mean(z * z, axis=-1, keepdims=True) + EPS) * g_ref[...]


def _out_proj(hm, yr, w_out, x, gate, final_g):
    B, T, D = x.shape
    tm = min(OUT_TM, T)
    per_b = gate.shape[0] == B
    row = lambda b, t: (b, t, 0)
    return pl.pallas_call(
        _out_kernel,
        grid=(B, T // tm),
        in_specs=[pl.BlockSpec((None, tm, M_WIDTH), row),
                  pl.BlockSpec((None, tm, R_WIDTH), row),
                  pl.BlockSpec((D, D), lambda b, t: (0, 0)),
                  pl.BlockSpec((None, tm, D), row),
                  pl.BlockSpec((None, 1, D), lambda b, t: (b if per_b else 0, 0, 0)),
                  pl.BlockSpec((1, D), lambda b, t: (0, 0))],
        out_specs=pl.BlockSpec((None, tm, D), row),
        out_shape=jax.ShapeDtypeStruct((B, T, D), F32),
        compiler_params=_params(2),
        name="out_proj",
    )(hm, yr, w_out, x, gate, final_g)


def _chain_param(p, n_b):
    q = p.reshape(HEAD_PAIRS, 2, R_N)
    if n_b == SUBLANES:
        t = jnp.broadcast_to(q.transpose(2, 1, 0)[..., None], (R_N, 2, HEAD_PAIRS, n_b))
        return t.reshape(1, R_N, LANES)
    t = jnp.broadcast_to(q.transpose(1, 2, 0)[..., None], (2, R_N, HEAD_PAIRS, n_b))
    return t.reshape(2, R_N, LANES)


def _chain_state(s, n_b):
    s6 = s.reshape(n_b, N_DIR, HEAD_PAIRS, 2, R_N, R_N)
    if n_b == SUBLANES:
        return s6.transpose(1, 5, 4, 3, 2, 0).reshape(N_DIR, 1, R_N, R_N, LANES)
    return s6.transpose(1, 3, 5, 4, 2, 0).reshape(N_DIR, 2, R_N, R_N, LANES)


def _unchain_state(s, n_b):
    if n_b == SUBLANES:
        s6 = s.reshape(N_DIR, R_N, R_N, 2, HEAD_PAIRS, n_b).transpose(5, 0, 4, 3, 2, 1)
    else:
        s6 = s.reshape(N_DIR, 2, R_N, R_N, HEAD_PAIRS, n_b).transpose(5, 0, 4, 1, 3, 2)
    return s6.reshape(n_b, N_DIR, R_HEADS, R_N, R_N)


def _block(x, mod, st, lw, norm_g, final_g, grid):
    (w_mproj, w_rproj, m_conv_w, m_conv_b, m_gate_b, m_ln_g, r_mu, r_w0, w_lora, r_a0,
     r_k_k, r_k_a, r_r_k, r_ln_g, r_ln_b, w_out) = lw
    C0, n0, m0, S0 = st
    B, T, D = x.shape
    assert B in (SUBLANES, 2 * SUBLANES), "chain layout is written for 8 or 16 sequences"
    shift, scale, gate = jnp.split(mod, 3, axis=-1)
    g2 = norm_g.reshape(1, D)
    p = _proj(x, scale, shift, g2, w_mproj, M_PROJ_COLS // 3).reshape(B, T, M_PROJ_COLS)
    h_tb = _modulated(x, g2, scale, shift).astype(BF16).transpose(1, 0, 2).reshape(T * B, D)
    pr = _mm(h_tb, w_rproj, min(PROJ_ROWS, T * B), R_PROJ_COLS // 2)
    pr = pr.reshape(T, B, R_PROJ_COLS)
    mg = p[..., 5 * M_WIDTH:5 * M_WIDTH + M_GATE_COLS]

    gates = (mg.reshape(B, T, N_DIR, 2, M_HEADS) + m_gate_b).transpose(2, 3, 0, 4, 1)
    nc = T // CHUNK
    log_i = gates[:, 0].reshape(N_DIR, B, M_HEADS, nc, CHUNK)
    log_f = jax.nn.log_sigmoid(gates[:, 1]).reshape(N_DIR, B, M_HEADS, nc, CHUNK)
    hm, Cn, nn, mn = _mlstm(
        p, m_conv_w, m_conv_b.reshape(1, 2 * M_WIDTH), m_ln_g.reshape(1, M_WIDTH), log_i, log_f,
        C0, n0[..., None, :], jnp.broadcast_to(m0[..., None, None], m0.shape + (1, LANES)))

    xs, xl = _rwkv_prep(pr, r_mu.reshape(1, SHIFT_COLS), w_lora, grid)
    p_dir = jnp.stack([jnp.stack([_chain_param(r_w0[z], B), _chain_param(r_a0[z], B)], axis=1)
                       for z in range(N_DIR)])
    p_sh = jnp.stack([_chain_param(r_k_k, B), _chain_param(r_k_a, B),
                      _chain_param(r_r_k.reshape(R_WIDTH), B)], axis=1)
    y_dirs, c_dirs, S_new = _rwkv(xs, xl, p_dir, p_sh, _chain_state(S0, B))
    G = p_sh.shape[0]
    ln = jnp.stack([_chain_param(r_ln_g, B), _chain_param(r_ln_b, B)])
    ln = ln.transpose(0, 2, 1, 3).reshape(2, R_N, G * LANES)
    yr = _rwkv_post(y_dirs, c_dirs, pr, ln)

    out = _out_proj(hm, yr, w_out, x, gate, final_g.reshape(1, D))
    return out, (Cn, nn[..., 0, :], mn[..., 0, 0], _unchain_state(S_new, B))


def kernel(x_prompt, x_sample, state_mlstm_C, state_mlstm_n, state_mlstm_m, state_rwkv_S, c, c_ctx,
           norm_g, w_ada, b_ada, w_in, m_conv_w, m_conv_b, m_gate_b, m_ln_g, r_mu, r_w0, r_w2, r_a0,
           r_a2, r_k_k, r_k_a, r_r_k, r_ln_g, r_ln_b, w_out, final_g):
    depth = w_in.shape[0]
    assert depth == 1, "the final RMSNorm is fused into the single layer's output projection"
    bp = x_prompt.shape[0]
    ctx_state0 = (jnp.zeros((bp, N_DIR, M_HEADS, M_DK, M_DK), F32),
                  jnp.zeros((bp, N_DIR, M_HEADS, M_DK), F32),
                  jnp.full((bp, N_DIR, M_HEADS), -jnp.inf, F32),
                  jnp.zeros((bp, N_DIR, R_HEADS, R_N, R_N), F32))
    l = 0
    w_l = w_in[l]
    gate_lo = 5 * M_WIDTH
    gate_hi = gate_lo + M_GATE_COLS
    w_mproj = jnp.concatenate(
        [w_l[:, :gate_hi], jnp.zeros((D_MODEL, GATE_PAD - M_GATE_COLS), F32)], axis=1).astype(BF16)
    w_rproj = w_l[:, gate_hi:].astype(BF16)
    zpad = jnp.zeros((LORA, R_WIDTH), F32)
    w_lora = jnp.stack([jnp.concatenate([r_w2[l, 0], zpad]), jnp.concatenate([zpad, r_w2[l, 1]]),
                        jnp.concatenate([r_a2[l, 0], zpad]), jnp.concatenate([zpad, r_a2[l, 1]])])
    lw = (w_mproj, w_rproj, m_conv_w[l], m_conv_b[l], m_gate_b[l], m_ln_g[l], r_mu[l], r_w0[l], w_lora,
          r_a0[l], r_k_k[l], r_k_a[l], r_r_k[l], r_ln_g[l], r_ln_b[l], w_out[l].astype(BF16))
    n_cond = 1 + c.shape[0]
    cond = jnp.concatenate([c_ctx[None, :], c, jnp.zeros((16 - n_cond, D_MODEL), F32)], axis=0)
    mod = _mm(jax.nn.silu(cond), w_ada[l], 16, 1536)[:n_cond] + b_ada[l]
    mod_p = mod[0:1, None, :]
    mod_s = mod[1:, None, :]
    y_prompt, (Cp, np_, mp, Sp) = _block(x_prompt, mod_p, ctx_state0, lw, norm_g[l], final_g, False)
    st_s = (state_mlstm_C[:, l], state_mlstm_n[:, l], state_mlstm_m[:, l], state_rwkv_S[:, l])
    y_sample, _ = _block(x_sample, mod_s, st_s, lw, norm_g[l], final_g, True)
    return (y_prompt, y_sample, Cp[:, None], np_[:, None], mp[:, None], Sp[:, None])
```

```python
import functools

import jax
import jax.numpy as jnp
import numpy as np
from jax import lax
from jax.experimental import pallas as pl
from jax.experimental.pallas import tpu as pltpu

D_MODEL = 2048
GRID_W = 64
N_DIR = 2
M_WIDTH = D_MODEL // 2
M_HEADS = 4
M_DK = M_WIDTH // M_HEADS
R_WIDTH = D_MODEL - M_WIDTH
R_N = 64
R_HEADS = R_WIDTH // R_N
LORA = 64
CONV_K = 3
CHUNK = 64
EPS = 1e-6
LNX_EPS = 64e-5
M_GATE_COLS = N_DIR * 2 * M_HEADS
SHIFT_COLS = 3 * R_WIDTH + 2 * N_DIR * LORA

LANES = 128
SUBLANES = 8
HEAD_PAIRS = R_WIDTH // LANES
GATE_PAD = 2 * LANES
M_PROJ_COLS = 5 * M_WIDTH + GATE_PAD
R_PROJ_COLS = R_WIDTH + SHIFT_COLS
RZ_BLK0 = 0
RS_BLK0 = R_WIDTH // LANES
VMEM_LIMIT = 48 * 1024 * 1024

F32 = jnp.float32
BF16 = jnp.bfloat16


def _params(n_axes, vmem_limit=VMEM_LIMIT):
    return pltpu.CompilerParams(dimension_semantics=("arbitrary",) * n_axes,
                                vmem_limit_bytes=vmem_limit)


def _sigmoid(x):
    return 1.0 / (1.0 + jnp.exp(-x))


def _silu(x):
    return x * _sigmoid(x)


def _softplus(x):
    return jnp.maximum(x, 0.0) + jnp.log(1.0 + jnp.exp(-jnp.abs(x)))


def _mm_kernel(x_ref, w_ref, o_ref):
    o_ref[...] = jnp.dot(x_ref[...].astype(BF16), w_ref[...].astype(BF16),
                         preferred_element_type=F32)


def _mm(x, w, tm, tn):
    M, K = x.shape
    N = w.shape[1]
    assert M % tm == 0 and N % tn == 0
    return pl.pallas_call(
        _mm_kernel,
        grid=(N // tn, M // tm),
        in_specs=[pl.BlockSpec((tm, K), lambda j, i: (i, 0)),
                  pl.BlockSpec((K, tn), lambda j, i: (0, j))],
        out_specs=pl.BlockSpec((tm, tn), lambda j, i: (i, j)),
        out_shape=jax.ShapeDtypeStruct((M, N), F32),
        compiler_params=_params(2),
        name="proj_mm",
    )(x, w)


PROJ_ROWS = 512


def _modulated(x, g, scale, shift):
    return x * lax.rsqrt(jnp.mean(x * x, axis=-1, keepdims=True) + EPS) * g * (1.0 + scale) + shift


def _proj_kernel(x_ref, sc_ref, sh_ref, g_ref, w_ref, o_ref):
    h = _modulated(x_ref[...], g_ref[...], sc_ref[...], sh_ref[...])
    o_ref[...] = jnp.dot(h.astype(BF16), w_ref[...], preferred_element_type=F32)


def _proj(x, scale, shift, norm_g, w, tn):
    B, T, D = x.shape
    N = w.shape[1]
    per_b = scale.shape[0] == B
    rows = min(PROJ_ROWS, T) if per_b else min(PROJ_ROWS, B * T)
    mod_spec = pl.BlockSpec((None, 1, D), lambda j, i: (i * rows // T if per_b else 0, 0, 0))
    return pl.pallas_call(
        _proj_kernel,
        grid=(N // tn, B * T // rows),
        in_specs=[pl.BlockSpec((rows, D), lambda j, i: (i, 0)), mod_spec, mod_spec,
                  pl.BlockSpec((1, D), lambda j, i: (0, 0)),
                  pl.BlockSpec((D, tn), lambda j, i: (0, j))],
        out_specs=pl.BlockSpec((rows, tn), lambda j, i: (i, j)),
        out_shape=jax.ShapeDtypeStruct((B * T, N), F32),
        compiler_params=_params(2),
        name="proj_bm",
    )(x.reshape(B * T, D), scale, shift, norm_g, w)


MLSTM_UNROLL = 4
MLSTM_VMEM_LIMIT = 56 * 1024 * 1024


def _mlstm_kernel(q_ref, k_ref, v_ref, o_ref, z_ref, wq_ref, wk_ref, bq_ref, bk_ref, lng_ref,
                  gi_ref, gf_ref, c0_ref, n0_ref, m0_ref,
                  hm_ref, c_ref, n_ref, m_ref, qs, ks, hs_f, hs_b, bcol_s, icol_s, rmax_s, rows_s):
    n_tok = q_ref.shape[0]
    nc = n_tok // CHUNK
    L = CHUNK
    tpos = lax.broadcasted_iota(jnp.int32, (n_tok, M_DK), 0)

    def conv_silu(x_ref, w_ref, b_ref):
        x = x_ref[...]
        prev = jnp.where(tpos == 0, 0.0, pltpu.roll(x, 1, 0))
        nxt = jnp.where(tpos == n_tok - 1, 0.0, pltpu.roll(x, n_tok - 1, 0))
        return _silu(prev * w_ref[0:1, :] + x * w_ref[1:2, :] + nxt * w_ref[2:3, :] + b_ref[...])

    qs[...] = conv_silu(q_ref, wq_ref, bq_ref)
    ks[...] = conv_silu(k_ref, wk_ref, bk_ref) * (M_DK ** -0.5)
    c_ref[...] = c0_ref[...]

    row = lax.broadcasted_iota(jnp.int32, (L, L), 0)
    col = lax.broadcasted_iota(jnp.int32, (L, L), 1)
    eye = row == col
    neg_inf = jnp.float32(-jnp.inf)

    def masks(dirn):
        seen = col <= row if dirn == 0 else col >= row
        seen_t = row <= col if dirn == 0 else row >= col
        return seen, seen_t

    def gate_stats(dirn, cc):
        seen, seen_t = masks(dirn)
        idx = dirn * nc + cc
        f_row = gf_ref[dirn, pl.ds(cc, 1), :]
        i_row = gi_ref[dirn, pl.ds(cc, 1), :]
        f_col = jnp.sum(jnp.where(eye, f_row, 0.0), axis=1, keepdims=True)
        i_col = jnp.sum(jnp.where(eye, i_row, 0.0), axis=1, keepdims=True)
        b_col = jnp.sum(jnp.where(seen, f_row, 0.0), axis=1, keepdims=True)
        b_row = jnp.sum(jnp.where(seen_t, f_col, 0.0), axis=0, keepdims=True)
        dmat = jnp.where(seen, b_col - b_row + i_row, neg_inf)
        b_last = jnp.sum(f_row, axis=1, keepdims=True)
        g_row = b_last - b_row + i_row
        bcol_s[idx] = jnp.broadcast_to(b_col, (L, LANES))
        icol_s[idx] = jnp.broadcast_to(i_col, (L, LANES))
        rmax_s[idx] = jnp.broadcast_to(jnp.max(dmat, axis=1, keepdims=True), (L, LANES))
        rows_s[idx, 0:1, 0:L] = b_row
        rows_s[idx, 1:2, 0:L] = i_row
        rows_s[idx, 2:3, :] = jnp.broadcast_to(b_last, (1, LANES))
        rows_s[idx, 3:4, :] = jnp.broadcast_to(jnp.max(g_row, axis=1, keepdims=True), (1, LANES))

    def stats_body(cc, carry):
        gate_stats(0, cc)
        gate_stats(1, cc)
        return carry

    lax.fori_loop(0, nc, stats_body, 0, unroll=min(4, nc))

    def wide(x):
        return jnp.concatenate([x] * (M_DK // LANES), axis=1)

    def chunk(dirn, cc, n, m):
        seen, _ = masks(dirn)
        idx = dirn * nc + cc
        r0 = pl.multiple_of(cc * L, L)
        q = qs[pl.ds(r0, L), :]
        k = ks[pl.ds(r0, L), :]
        v = v_ref[pl.ds(r0, L), :]
        b_col = bcol_s[idx]
        b_row = rows_s[idx, 0:1, 0:L]
        i_row = rows_s[idx, 1:2, 0:L]
        b_last = rows_s[idx, 2:3, :]
        inter = b_col + m
        mt = jnp.maximum(inter, rmax_s[idx])
        dmat = jnp.where(seen, b_col[:, :L] - b_row + i_row, neg_inf)
        qb = q.astype(BF16)
        kb = k.astype(BF16)
        vb = v.astype(BF16)
        qk = lax.dot_general(qb, kb, (((1,), (1,)), ((), ())), preferred_element_type=F32)
        a = jnp.exp(dmat - mt[:, :L]) * qk
        s_in = jnp.exp(inter - mt)
        c = c_ref[dirn]
        qc = jnp.dot(qb, c.astype(BF16), preferred_element_type=F32)
        av = jnp.dot(a.astype(BF16), vb, preferred_element_type=F32)
        num = wide(s_in) * qc + av
        den = s_in * jnp.sum(q * n, axis=1, keepdims=True) + jnp.sum(a, axis=1, keepdims=True)
        hs = hs_f if dirn == 0 else hs_b
        hs[pl.ds(r0, L), :] = num / wide(jnp.maximum(jnp.abs(den), jnp.exp(-mt)))
        m_new = jnp.maximum(b_last + m, rows_s[idx, 3:4, :])
        decay = wide(jnp.exp(b_last + m - m_new))
        g_col = b_last - b_col + icol_s[idx]
        wk = wide(jnp.exp(g_col - m_new)) * k
        c_ref[dirn] = decay * c + lax.dot_general(
            wk.astype(BF16), vb, (((0,), (0,)), ((), ())), preferred_element_type=F32)
        n_new = decay * n + jnp.sum(wk, axis=0, keepdims=True)
        return n_new, m_new

    def body(ci, carry):
        n_f, m_f, n_b, m_b = carry
        n_f, m_f = chunk(0, ci, n_f, m_f)
        n_b, m_b = chunk(1, nc - 1 - ci, n_b, m_b)
        return n_f, m_f, n_b, m_b

    n_f, m_f, n_b, m_b = lax.fori_loop(
        0, nc, body, (n0_ref[0], m0_ref[0], n0_ref[1], m0_ref[1]), unroll=MLSTM_UNROLL)
    n_ref[0] = n_f
    n_ref[1] = n_b
    m_ref[0] = m_f
    m_ref[1] = m_b

    hm = _sigmoid(o_ref[...]) * (hs_f[...] + hs_b[...])
    hc = hm - jnp.mean(hm, axis=1, keepdims=True)
    hn = hc * lax.rsqrt(jnp.mean(hc * hc, axis=1, keepdims=True) + EPS)
    hm_ref[...] = (hn * lng_ref[...] * _silu(z_ref[...])).astype(BF16)


def _mlstm(p, conv_w, conv_b, ln_g, gi, gf, c0, n0, m0):
    B, T, _ = p.shape
    nc = T // CHUNK
    tiles_per_part = M_WIDTH // M_DK

    def col_spec(part):
        return pl.BlockSpec((None, T, M_DK), lambda b, h: (b, 0, part * tiles_per_part + h))

    def par_spec(rows, part):
        return pl.BlockSpec((rows, M_DK), lambda b, h: (0, part * tiles_per_part + h))

    gate_spec = pl.BlockSpec((N_DIR, None, None, nc, CHUNK), lambda b, h: (0, b, h, 0, 0))
    c_spec = pl.BlockSpec((None, N_DIR, None, M_DK, M_DK), lambda b, h: (b, 0, h, 0, 0))
    n_spec = pl.BlockSpec((None, N_DIR, None, 1, M_DK), lambda b, h: (b, 0, h, 0, 0))
    m_spec = pl.BlockSpec((None, N_DIR, None, 1, LANES), lambda b, h: (b, 0, h, 0, 0))
    seq = pltpu.VMEM((T, M_DK), F32)
    stat = pltpu.VMEM((N_DIR * nc, CHUNK, LANES), F32)
    return pl.pallas_call(
        _mlstm_kernel,
        grid=(B, M_HEADS),
        in_specs=[col_spec(0), col_spec(1), col_spec(2), col_spec(3), col_spec(4),
                  par_spec(CONV_K, 0), par_spec(CONV_K, 1), par_spec(1, 0), par_spec(1, 1),
                  par_spec(1, 0), gate_spec, gate_spec, c_spec, n_spec, m_spec],
        out_specs=[pl.BlockSpec((None, T, M_DK), lambda b, h: (b, 0, h)), c_spec, n_spec, m_spec],
        out_shape=[jax.ShapeDtypeStruct((B, T, M_WIDTH), BF16),
                   jax.ShapeDtypeStruct((B, N_DIR, M_HEADS, M_DK, M_DK), F32),
                   jax.ShapeDtypeStruct((B, N_DIR, M_HEADS, 1, M_DK), F32),
                   jax.ShapeDtypeStruct((B, N_DIR, M_HEADS, 1, LANES), F32)],
        scratch_shapes=[seq, seq, seq, seq, stat, stat, stat,
                        pltpu.VMEM((N_DIR * nc, SUBLANES, LANES), F32)],
        compiler_params=_params(2, MLSTM_VMEM_LIMIT),
        name="mlstm_scan",
    )(p, p, p, p, p, conv_w, conv_w, conv_b, conv_b, ln_g, gi, gf, c0, n0, m0)


MIX_SLOTS = 3
LORA_SLOTS = 2 * N_DIR
MIX_TT = 1024
LORA_TT = 128
MIX_TILE0 = SHIFT_COLS // LANES - 2


def _prep_kernel(x_ref, lo_ref, hi_ref, mu_ref, *rest, grid_shift, n_tok, lora):
    tblk = pl.program_id(0)
    c = pl.program_id(1)
    x = x_ref[...]
    tt, n_b, _ = x.shape
    t = tblk * tt + lax.broadcasted_iota(jnp.int32, x.shape, 0)
    lane4 = lax.broadcasted_iota(jnp.int32, x.shape, 2) % 4
    xe = jnp.concatenate([lo_ref[...], x, hi_ref[...]], axis=0)
    prev = xe[GRID_W - 1:GRID_W - 1 + tt]
    nxt = xe[GRID_W + 1:GRID_W + 1 + tt]
    if grid_shift:
        col = t % GRID_W
        left = jnp.where(col == 0, 0.0, prev)
        right = jnp.where(col == GRID_W - 1, 0.0, nxt)
        up = jnp.where(t < GRID_W, 0.0, xe[:tt])
        down = jnp.where(t >= n_tok - GRID_W, 0.0, xe[2 * GRID_W:])
        sh = jnp.where(lane4 == 0, left, jnp.where(lane4 == 1, right, jnp.where(lane4 == 2, up, down)))
    else:
        sh = jnp.where(lane4 % 2 == 0, jnp.where(t == 0, 0.0, prev), jnp.where(t == n_tok - 1, 0.0, nxt))
    mixed = x + mu_ref[...] * (sh - x)

    def emit(val, dst):
        v4 = val.reshape(tt // 2, 2, n_b, LANES)
        even, odd = v4[:, 0], v4[:, 1]
        if n_b == SUBLANES:
            lo = lax.broadcasted_iota(jnp.int32, even.shape, 2) < R_N
            dst(0, jnp.where(lo, even, pltpu.roll(odd, R_N, 2)))
            dst(1, jnp.where(lo, pltpu.roll(even, R_N, 2), odd))
        else:
            dst(0, even)
            dst(1, odd)

    if not lora:
        (o_ref,) = rest

        def dst(row, val):
            o_ref[:, row] = val

        emit(mixed, dst)
    else:
        w_ref, o_ref = rest
        act = jnp.where(c < N_DIR, jnp.tanh(mixed), mixed)
        prod = jnp.dot(act.reshape(tt * n_b, LANES).astype(BF16), w_ref[...].astype(BF16),
                       preferred_element_type=F32)
        for hp in range(HEAD_PAIRS):
            def dst(row, val, hp=hp):
                o_ref[:, row, hp] = val

            emit(prod[:, hp * LANES:(hp + 1) * LANES].reshape(tt, n_b, LANES), dst)


def _rwkv_prep(p, r_mu, w_lora, grid_shift):
    T, B, _ = p.shape
    n_halo = T // GRID_W

    def specs(tt, col):
        per_blk = tt // GRID_W
        return [pl.BlockSpec((tt, B, LANES), lambda t, c: (t, 0, RS_BLK0 + col(c))),
                pl.BlockSpec((GRID_W, B, LANES),
                             lambda t, c: (jnp.maximum(t * per_blk - 1, 0), 0, RS_BLK0 + col(c))),
                pl.BlockSpec((GRID_W, B, LANES),
                             lambda t, c: (jnp.minimum((t + 1) * per_blk, n_halo - 1), 0, RS_BLK0 + col(c))),
                pl.BlockSpec((1, LANES), lambda t, c: (0, col(c)))]

    tt = min(MIX_TT, T)
    mix = pl.pallas_call(
        functools.partial(_prep_kernel, grid_shift=grid_shift, n_tok=T, lora=False),
        grid=(T // tt, MIX_SLOTS * HEAD_PAIRS),
        in_specs=specs(tt, lambda c: c),
        out_specs=pl.BlockSpec((None, tt // 2, 2, None, B, LANES),
                               lambda t, c: (c // HEAD_PAIRS, t, 0, c % HEAD_PAIRS, 0, 0)),
        out_shape=jax.ShapeDtypeStruct((MIX_SLOTS, T // 2, 2, HEAD_PAIRS, B, LANES), F32),
        compiler_params=_params(2),
        name="rwkv_prep",
    )(p, p, p, r_mu)
    tt = min(LORA_TT, T)
    lora = pl.pallas_call(
        functools.partial(_prep_kernel, grid_shift=grid_shift, n_tok=T, lora=True),
        grid=(T // tt, LORA_SLOTS),
        in_specs=specs(tt, lambda c: MIX_TILE0 + c // N_DIR)
        + [pl.BlockSpec((None, LANES, R_WIDTH), lambda t, c: (c, 0, 0))],
        out_specs=pl.BlockSpec((None, tt // 2, 2, HEAD_PAIRS, B, LANES),
                               lambda t, c: (c, t, 0, 0, 0, 0)),
        out_shape=jax.ShapeDtypeStruct((LORA_SLOTS, T // 2, 2, HEAD_PAIRS, B, LANES), F32),
        compiler_params=_params(2),
        name="rwkv_lora",
    )(p, p, p, r_mu, w_lora)
    return mix, lora


RWKV_TB = 32
J_UNROLL = 16
PREP_UNROLL = 8


def _chain_tiles(ref, pair, group, n_b):
    if n_b == SUBLANES:
        xt = ref[pair].reshape(LANES, LANES).T
        return [xt[:R_N], xt[R_N:]]
    tiles = []
    for parity in range(2):
        xt = ref[pair, parity].reshape(LANES, LANES).T
        tiles.append(jnp.where(group == 0, xt[:R_N], xt[R_N:]))
    return tiles


def _rwkv_kernel(r_ref, k_ref, v_ref, lw_ref, la_ref, pd_ref, ps_ref, s0_ref,
                 y_ref, c_ref, s_ref, kh_s, rh_s, vv_s, kt_s, bt_s, g_s, *, n_b):
    dirn = pl.program_id(0)
    group = pl.program_id(1)
    tb = pl.program_id(2)
    n_tok = y_ref.shape[0]
    n_pairs = n_tok // 2
    fwd = dirn == 0

    @pl.when(tb == 0)
    def _():
        s_ref[...] = s0_ref[...]

    w0, a0 = pd_ref[0], pd_ref[1]
    k_k, k_a, r_k = ps_ref[0], ps_ref[1], ps_ref[2]

    def prep(q, g):
        pair = q + dirn * (n_pairs - 1 - 2 * q)
        tiles = [_chain_tiles(ref, pair, group, n_b) for ref in (r_ref, k_ref, v_ref, lw_ref, la_ref)]
        for i in range(2):
            r, kraw, v, lw, la = (jnp.where(fwd, x[i], x[1 - i]) for x in tiles)
            pos = 2 * q + i
            tok = 2 * pair + i + dirn * (1 - 2 * i)
            wlog = -_softplus(-(w0 + lw)) - 0.5
            a = _sigmoid(a0 + la)
            kk = kraw * k_k
            kk = kk / jnp.maximum(jnp.sqrt(jnp.sum(kk * kk, axis=0, keepdims=True)), 1e-12)
            kz = kraw * (1.0 + (a - 1.0) * k_a)
            c_ref[tok] = jnp.sum(r * kz * r_k, axis=0, keepdims=True) * v
            kh_s[pos] = kk * g
            g = g * jnp.exp(-jnp.exp(wlog))
            g_inv = 1.0 / g
            vv_s[pos] = v
            bt_s[pos] = kk * a * g_inv
            kt_s[pos] = kz * g_inv
            rh_s[pos] = r * g
        return g

    g_s[...] = lax.fori_loop(0, n_pairs, prep, jnp.ones((R_N, LANES), F32), unroll=PREP_UNROLL)

    zeros = jnp.zeros((R_N, LANES), F32)

    def dot_first(jb, u):
        for jj in range(J_UNROLL):
            j = jb * J_UNROLL + jj
            u = u + s_ref[j] * kh_s[0, pl.ds(j, 1), :]
        return u

    def step(pos, u):
        tok = pos + dirn * (n_tok - 1 - 2 * pos)
        nxt = jnp.minimum(pos + 1, n_tok - 1)
        sa = -u
        vt = vv_s[pos]

        def update(jb, carry):
            y, u_next = carry
            for jj in range(J_UNROLL):
                j = jb * J_UNROLL + jj
                sn = s_ref[j] + sa * bt_s[pos, pl.ds(j, 1), :] + vt * kt_s[pos, pl.ds(j, 1), :]
                s_ref[j] = sn
                y = y + sn * rh_s[pos, pl.ds(j, 1), :]
                u_next = u_next + sn * kh_s[nxt, pl.ds(j, 1), :]
            return y, u_next

        y, u_next = lax.fori_loop(0, R_N // J_UNROLL, update, (zeros, zeros))
        y_ref[tok] = y
        return u_next

    lax.fori_loop(0, n_tok, step, lax.fori_loop(0, R_N // J_UNROLL, dot_first, zeros))

    def rescale(jb, carry):
        for jj in range(J_UNROLL):
            j = jb * J_UNROLL + jj
            s_ref[j] = s_ref[j] * g_s[pl.ds(j, 1), :]
        return carry

    lax.fori_loop(0, R_N // J_UNROLL, rescale, 0)


def _rwkv(xs, xl, p_dir, p_sh, s0):
    _, half_t, _, _, B, _ = xs.shape
    T = 2 * half_t
    G = p_sh.shape[0]
    tb = min(RWKV_TB, T)
    nt = T // tb
    rev = lambda dd, t: t + dd * (nt - 1 - 2 * t)

    def slot(s):
        return pl.BlockSpec((None, tb // 2, 2, HEAD_PAIRS, B, LANES),
                            lambda dd, g, t: (s(dd), rev(dd, t), 0, 0, 0, 0))

    state = pl.BlockSpec((None, None, R_N, R_N, LANES), lambda dd, g, t: (dd, g, 0, 0, 0))
    out = pl.BlockSpec((None, tb, R_N, LANES), lambda dd, g, t: (dd, rev(dd, t), 0, g))
    scratch = pltpu.VMEM((tb, R_N, LANES), F32)
    return pl.pallas_call(
        functools.partial(_rwkv_kernel, n_b=B),
        grid=(N_DIR, G, nt),
        in_specs=[slot(lambda dd: 0), slot(lambda dd: 1), slot(lambda dd: 2),
                  slot(lambda dd: dd), slot(lambda dd: N_DIR + dd),
                  pl.BlockSpec((None, None, 2, R_N, LANES), lambda dd, g, t: (dd, g, 0, 0, 0)),
                  pl.BlockSpec((None, 3, R_N, LANES), lambda dd, g, t: (g, 0, 0, 0)),
                  state],
        out_specs=[out, out, state],
        out_shape=[jax.ShapeDtypeStruct((N_DIR, T, R_N, G * LANES), F32),
                   jax.ShapeDtypeStruct((N_DIR, T, R_N, G * LANES), F32),
                   jax.ShapeDtypeStruct((N_DIR, G, R_N, R_N, LANES), F32)],
        scratch_shapes=[scratch] * 5 + [pltpu.VMEM((R_N, LANES), F32)],
        compiler_params=_params(3),
        name="rwkv_scan",
    )(xs, xs, xs, xl, xl, p_dir, p_sh, s0)


def _post_kernel(y_ref, c_ref, z_ref, ln_ref, o_ref, nat_s, *, n_b):
    n_tok = y_ref.shape[1]
    lo = lax.broadcasted_iota(jnp.int32, (R_N, LANES), 1) < R_N

    def normed(t, g):
        lanes = pl.ds(g * LANES, LANES)
        y = y_ref[0, t, :, lanes] + y_ref[1, t, :, lanes]
        mean = jnp.mean(y, axis=0, keepdims=True)
        yc = y - mean
        var = jnp.mean(yc * yc, axis=0, keepdims=True)
        return (yc * lax.rsqrt(var + LNX_EPS) * ln_ref[0, :, lanes] + ln_ref[1, :, lanes]
                + c_ref[0, t, :, lanes] + c_ref[1, t, :, lanes])

    rows_per_tok = HEAD_PAIRS * n_b

    def gate(t, x):
        z = z_ref[t]
        for hp in range(HEAD_PAIRS):
            r0 = pl.multiple_of(t * rows_per_tok + hp * n_b, SUBLANES)
            nat_s[pl.ds(r0, n_b), :] = x[hp * n_b:(hp + 1) * n_b] * _silu(z[:, hp * LANES:(hp + 1) * LANES])

    def body(tp, carry):
        t0 = tp * 2
        if n_b == SUBLANES:
            a, b = normed(t0, 0), normed(t0 + 1, 0)
            xt = jnp.concatenate([jnp.where(lo, a, pltpu.roll(b, R_N, 1)),
                                  jnp.where(lo, pltpu.roll(a, R_N, 1), b)], axis=0)
            x = xt.T
            for i in range(2):
                gate(t0 + i, x[i * R_N:(i + 1) * R_N])
        else:
            for i in range(2):
                gate(t0 + i, jnp.concatenate([normed(t0 + i, 0), normed(t0 + i, 1)], axis=0).T)
        return carry

    lax.fori_loop(0, n_tok // 2, body, 0, unroll=2)

    for hp in range(HEAD_PAIRS):
        for b in range(n_b):
            o_ref[b, :, hp * LANES:(hp + 1) * LANES] = nat_s[pl.ds(hp * n_b + b, n_tok, stride=rows_per_tok), :]


def _rwkv_post(y, c, p, ln):
    _, T, _, CH = y.shape
    B = p.shape[1]
    tb = min(RWKV_TB, T)
    chain = pl.BlockSpec((N_DIR, tb, R_N, CH), lambda t: (0, t, 0, 0))
    nat = pl.BlockSpec((tb, B, R_WIDTH), lambda t: (t, 0, RZ_BLK0))
    return pl.pallas_call(
        functools.partial(_post_kernel, n_b=B),
        grid=(T // tb,),
        in_specs=[chain, chain, nat, pl.BlockSpec((2, R_N, CH), lambda t: (0, 0, 0))],
        out_specs=pl.BlockSpec((B, tb, R_WIDTH), lambda t: (0, t, 0)),
        out_shape=jax.ShapeDtypeStruct((B, T, R_WIDTH), F32),
        scratch_shapes=[pltpu.VMEM((tb * HEAD_PAIRS * B, LANES), F32)],
        compiler_params=_params(1),
        name="rwkv_post",
    )(y, c, p, ln)


OUT_TM = 512


def _out_kernel(hm_ref, yr_ref, w_ref, x_ref, gate_ref, g_ref, o_ref):
    cat = jnp.concatenate([hm_ref[...], yr_ref[...].astype(BF16)], axis=1)
    out = jnp.dot(cat, w_ref[...], preferred_element_type=F32)
    z = x_ref[...] + gate_ref[...] * out
    o_ref[...] = z * lax.rsqrt(jnp.mean(z * z, axis=-1, keepdims=True) + EPS) * g_ref[...]


def _out_proj(hm, yr, w_out, x, gate, final_g):
    B, T, D = x.shape
    tm = min(OUT_TM, T)
    per_b = gate.shape[0] == B
    row = lambda b, t: (b, t, 0)
    return pl.pallas_call(
        _out_kernel,
        grid=(B, T // tm),
        in_specs=[pl.BlockSpec((None, tm, M_WIDTH), row),
                  pl.BlockSpec((None, tm, R_WIDTH), row),
                  pl.BlockSpec((D, D), lambda b, t: (0, 0)),
                  pl.BlockSpec((None, tm, D), row),
                  pl.BlockSpec((None, 1, D), lambda b, t: (b if per_b else 0, 0, 0)),
                  pl.BlockSpec((1, D), lambda b, t: (0, 0))],
        out_specs=pl.BlockSpec((None, tm, D), row),
        out_shape=jax.ShapeDtypeStruct((B, T, D), F32),
        compiler_params=_params(2),
        name="out_proj",
    )(hm, yr, w_out, x, gate, final_g)


def _chain_param(p, n_b):
    q = p.reshape(HEAD_PAIRS, 2, R_N)
    if n_b == SUBLANES:
        t = jnp.broadcast_to(q.transpose(2, 1, 0)[..., None], (R_N, 2, HEAD_PAIRS, n_b))
        return t.reshape(1, R_N, LANES)
    t = jnp.broadcast_to(q.transpose(1, 2, 0)[..., None], (2, R_N, HEAD_PAIRS, n_b))
    return t.reshape(2, R_N, LANES)


def _chain_state(s, n_b):
    s6 = s.reshape(n_b, N_DIR, HEAD_PAIRS, 2, R_N, R_N)
    if n_b == SUBLANES:
        return s6.transpose(1, 5, 4, 3, 2, 0).reshape(N_DIR, 1, R_N, R_N, LANES)
    return s6.transpose(1, 3, 5, 4, 2, 0).reshape(N_DIR, 2, R_N, R_N, LANES)


def _unchain_state(s, n_b):
    if n_b == SUBLANES:
        s6 = s.reshape(N_DIR, R_N, R_N, 2, HEAD_PAIRS, n_b).transpose(5, 0, 4, 3, 2, 1)
    else:
        s6 = s.reshape(N_DIR, 2, R_N, R_N, HEAD_PAIRS, n_b).transpose(5, 0, 4, 1, 3, 2)
    return s6.reshape(n_b, N_DIR, R_HEADS, R_N, R_N)


def _block(x, mod, st, lw, norm_g, final_g, grid):
    (w_mproj, w_rproj, m_conv_w, m_conv_b, m_gate_b, m_ln_g, r_mu, r_w0, w_lora, r_a0,
     r_k_k, r_k_a, r_r_k, r_ln_g, r_ln_b, w_out) = lw
    C0, n0, m0, S0 = st
    B, T, D = x.shape
    assert B in (SUBLANES, 2 * SUBLANES), "chain layout is written for 8 or 16 sequences"
    shift, scale, gate = jnp.split(mod, 3, axis=-1)
    g2 = norm_g.reshape(1, D)
    p = _proj(x, scale, shift, g2, w_mproj, M_PROJ_COLS // 3).reshape(B, T, M_PROJ_COLS)
    h_tb = _modulated(x, g2, scale, shift).astype(BF16).transpose(1, 0, 2).reshape(T * B, D)
    pr = _mm(h_tb, w_rproj, min(PROJ_ROWS, T * B), R_PROJ_COLS // 2)
    pr = pr.reshape(T, B, R_PROJ_COLS)
    mg = p[..., 5 * M_WIDTH:5 * M_WIDTH + M_GATE_COLS]

    gates = (mg.reshape(B, T, N_DIR, 2, M_HEADS) + m_gate_b).transpose(2, 3, 0, 4, 1)
    nc = T // CHUNK
    log_i = gates[:, 0].reshape(N_DIR, B, M_HEADS, nc, CHUNK)
    log_f = jax.nn.log_sigmoid(gates[:, 1]).reshape(N_DIR, B, M_HEADS, nc, CHUNK)
    hm, Cn, nn, mn = _mlstm(
        p, m_conv_w, m_conv_b.reshape(1, 2 * M_WIDTH), m_ln_g.reshape(1, M_WIDTH), log_i, log_f,
        C0, n0[..., None, :], jnp.broadcast_to(m0[..., None, None], m0.shape + (1, LANES)))

    xs, xl = _rwkv_prep(pr, r_mu.reshape(1, SHIFT_COLS), w_lora, grid)
    p_dir = jnp.stack([jnp.stack([_chain_param(r_w0[z], B), _chain_param(r_a0[z], B)], axis=1)
                       for z in range(N_DIR)])
    p_sh = jnp.stack([_chain_param(r_k_k, B), _chain_param(r_k_a, B),
                      _chain_param(r_r_k.reshape(R_WIDTH), B)], axis=1)
    y_dirs, c_dirs, S_new = _rwkv(xs, xl, p_dir, p_sh, _chain_state(S0, B))
    G = p_sh.shape[0]
    ln = jnp.stack([_chain_param(r_ln_g, B), _chain_param(r_ln_b, B)])
    ln = ln.transpose(0, 2, 1, 3).reshape(2, R_N, G * LANES)
    yr = _rwkv_post(y_dirs, c_dirs, pr, ln)

    out = _out_proj(hm, yr, w_out, x, gate, final_g.reshape(1, D))
    return out, (Cn, nn[..., 0, :], mn[..., 0, 0], _unchain_state(S_new, B))


def kernel(x_prompt, x_sample, state_mlstm_C, state_mlstm_n, state_mlstm_m, state_rwkv_S, c, c_ctx,
           norm_g, w_ada, b_ada, w_in, m_conv_w, m_conv_b, m_gate_b, m_ln_g, r_mu, r_w0, r_w2, r_a0,
           r_a2, r_k_k, r_k_a, r_r_k, r_ln_g, r_ln_b, w_out, final_g):
    depth = w_in.shape[0]
    assert depth == 1, "the final RMSNorm is fused into the single layer's output projection"
    bp = x_prompt.shape[0]
    ctx_state0 = (jnp.zeros((bp, N_DIR, M_HEADS, M_DK, M_DK), F32),
                  jnp.zeros((bp, N_DIR, M_HEADS, M_DK), F32),
                  jnp.full((bp, N_DIR, M_HEADS), -jnp.inf, F32),
                  jnp.zeros((bp, N_DIR, R_HEADS, R_N, R_N), F32))
    l = 0
    w_l = w_in[l]
    gate_lo = 5 * M_WIDTH
    gate_hi = gate_lo + M_GATE_COLS
    w_mproj = jnp.concatenate(
        [w_l[:, :gate_hi], jnp.zeros((D_MODEL, GATE_PAD - M_GATE_COLS), F32)], axis=1).astype(BF16)
    w_rproj = w_l[:, gate_hi:].astype(BF16)
    zpad = jnp.zeros((LORA, R_WIDTH), F32)
    w_lora = jnp.stack([jnp.concatenate([r_w2[l, 0], zpad]), jnp.concatenate([zpad, r_w2[l, 1]]),
                        jnp.concatenate([r_a2[l, 0], zpad]), jnp.concatenate([zpad, r_a2[l, 1]])])
    lw = (w_mproj, w_rproj, m_conv_w[l], m_conv_b[l], m_gate_b[l], m_ln_g[l], r_mu[l], r_w0[l], w_lora,
          r_a0[l], r_k_k[l], r_k_a[l], r_r_k[l], r_ln_g[l], r_ln_b[l], w_out[l].astype(BF16))
    n_cond = 1 + c.shape[0]
    cond = jnp.concatenate([c_ctx[None, :], c, jnp.zeros((16 - n_cond, D_MODEL), F32)], axis=0)
    mod = _mm(jax.nn.silu(cond), w_ada[l], 16, 1536)[:n_cond] + b_ada[l]
    mod_p = mod[0:1, None, :]
    mod_s = mod[1:, None, :]
    y_prompt, (Cp, np_, mp, Sp) = _block(x_prompt, mod_p, ctx_state0, lw, norm_g[l], final_g, False)
    st_s = (state_mlstm_C[:, l], state_mlstm_n[:, l], state_mlstm_m[:, l], state_rwkv_S[:, l])
    y_sample, _ = _block(x_sample, mod_s, st_s, lw, norm_g[l], final_g, True)
    return (y_prompt, y_sample, Cp[:, None], np_[:, None], mp[:, None], Sp[:, None])
```
